```python
import math
import jax
import jax.numpy as jnp
from jax import lax
import numpy as np

D_MODEL = 4096
BATCH = 1
SEQ = 8192
DEPTH = 1
DEC_BATCH = 128
DEC_SEQ = 1
PAST_LEN = 2048
PAGE_SIZE = 128

HEAD_DIM = 128
ROT_DIM = HEAD_DIM // 4
ROPE_THETA = 500000.0
DIL_GROUPS = ((128, 1), (512, 4), (2048, 16))
N_GROUPS = len(DIL_GROUPS)
A_HEADS = 4
A_WIDTH = A_HEADS * HEAD_DIM
Q_BLOCK = 128
B_HEADS = 16
B_DK = 128
B_DV = 128
B_QK_WIDTH = B_HEADS * B_DK
B_V_WIDTH = B_HEADS * B_DV
B_CONV = 4
B_CONV_COLS = 2 * B_QK_WIDTH + B_V_WIDTH
DELTA_CHUNK = 64
FFN_CONV = 3
D_FF = 256 * ((8 * D_MODEL // 3 + 255) // 256)
EPS = 1e-6

A_COLS = N_GROUPS * 3 * A_WIDTH
OFF_BQKV = A_COLS
OFF_BZ = OFF_BQKV + B_CONV_COLS
OFF_BBETA = OFF_BZ + B_V_WIDTH
OFF_BALPHA = OFF_BBETA + B_HEADS
OFF_GATE = OFF_BALPHA + B_HEADS
IN_COLS = OFF_GATE + 2 * D_MODEL

kernel_name = "hybrid_dilated_swa_gated_deltanet_convffn_step"

F32 = jnp.float32


def rmsnorm(x, w):
    xf = x.astype(F32)
    y = xf * lax.rsqrt(jnp.mean(xf * xf, axis=-1, keepdims=True) + EPS)
    return (y * w.astype(F32)).astype(x.dtype)


def l2norm(x):
    xf = x.astype(F32)
    return xf * lax.rsqrt(jnp.sum(xf * xf, axis=-1, keepdims=True) + EPS)


def partial_rope(x, pos):
    inv = ROPE_THETA ** (-jnp.arange(0, ROT_DIM, 2, dtype=F32) / ROT_DIM)
    ang = pos.astype(F32)[:, None] * inv[None, :]
    shape = (1, x.shape[1]) + (1,) * (x.ndim - 3) + (ROT_DIM // 2,)
    cos = jnp.cos(ang).reshape(shape)
    sin = jnp.sin(ang).reshape(shape)
    xr = x[..., :ROT_DIM].astype(F32)
    x1, x2 = xr[..., :ROT_DIM // 2], xr[..., ROT_DIM // 2:]
    rot = jnp.concatenate([x1 * cos - x2 * sin, x2 * cos + x1 * sin], axis=-1)
    return jnp.concatenate([rot.astype(x.dtype), x[..., ROT_DIM:]], axis=-1)


def causal_depthwise_conv(buf, x, w):
    width = w.shape[0]
    t = x.shape[1]
    xp = jnp.concatenate([buf.astype(x.dtype), x], axis=1)
    out = xp[:, 0:t] * w[0]
    for i in range(1, width):
        out = out + xp[:, i:i + t] * w[i]
    return out, xp[:, xp.shape[1] - (width - 1):]


def dilated_group_attention(q, k_all, v_all, n_before, window, dil):
    b, t, h, d = q.shape
    n_keys = window // dil + 1
    blk = Q_BLOCK if t % Q_BLOCK == 0 else t
    offs = dil * jnp.arange(n_keys, dtype=jnp.int32)
    qf, kf, vf = q.astype(F32), k_all.astype(F32), v_all.astype(F32)

    def one_block(start):
        qb = lax.dynamic_slice_in_dim(qf, start, blk, axis=1)
        rows = n_before + start + jnp.arange(blk, dtype=jnp.int32)
        idx = rows[:, None] - offs[None, :]
        valid = idx >= 0
        idx = jnp.maximum(idx, 0)
        kb = jnp.take(kf, idx, axis=1)
        vb = jnp.take(vf, idx, axis=1)
        s = jnp.einsum('bqhd,bqkhd->bhqk', qb, kb) * (HEAD_DIM ** -0.5)
        s = jnp.where(valid[None, None], s, -jnp.inf)
        m = jnp.max(s, axis=-1, keepdims=True)
        p = jnp.exp(s - m)
        den = jnp.sum(p, axis=-1, keepdims=True)
        o = jnp.einsum('bhqk,bqkhd->bqhd', p / den, vb)
        lse = (m + jnp.log(den))[..., 0]
        return o, jnp.transpose(lse, (0, 2, 1))

    starts = jnp.arange(t // blk, dtype=jnp.int32) * blk
    o, lse = lax.map(one_block, starts)
    o = jnp.moveaxis(o, 0, 1).reshape(b, t, h, d)
    lse = jnp.moveaxis(lse, 0, 1).reshape(b, t, h)
    return o, lse


def gated_delta_chunked(q, k, v, beta, g, s0):
    b, t, h, dk = q.shape
    dv = v.shape[-1]
    c = min(DELTA_CHUNK, t)
    n = -(-t // c)
    pad = n * c - t

    def to_chunks(a):
        a = a.astype(F32)
        a = jnp.pad(a, [(0, 0), (0, pad)] + [(0, 0)] * (a.ndim - 2))
        a = a.reshape((b, n, c) + a.shape[2:])
        return jnp.moveaxis(a, 3, 1)

    q, k, v, beta, g = (to_chunks(a) for a in (q, k, v, beta, g))
    q = q * (dk ** -0.5)
    gc = jnp.cumsum(g, axis=-1)
    tril = jnp.tril(jnp.ones((c, c), dtype=bool))
    strict = jnp.tril(jnp.ones((c, c), dtype=bool), -1)
    decay = jnp.exp(jnp.where(tril, gc[..., :, None] - gc[..., None, :], -jnp.inf))
    kb = k * beta[..., None]
    a = jnp.where(strict, jnp.einsum('bhnid,bhnjd->bhnij', kb, k) * decay, 0.0)
    eye = jnp.eye(c, dtype=F32)
    tmat = lax.linalg.triangular_solve(eye + a, jnp.broadcast_to(eye, a.shape), left_side=True, lower=True)
    u = jnp.einsum('bhnij,bhnjd->bhnid', tmat, v * beta[..., None])
    w = jnp.einsum('bhnij,bhnjd->bhnid', tmat, kb * jnp.exp(gc)[..., None])
    qk = jnp.einsum('bhnid,bhnjd->bhnij', q, k) * decay
    qg = q * jnp.exp(gc)[..., None]
    kd = k * jnp.exp(gc[..., -1:] - gc)[..., None]
    glast = jnp.exp(gc[..., -1])

    def step(s, xs):
        u_c, w_c, qk_c, qg_c, kd_c, gl_c = xs
        v_new = u_c - jnp.einsum('bhik,bhkv->bhiv', w_c, s)
        o_c = jnp.einsum('bhik,bhkv->bhiv', qg_c, s) + jnp.einsum('bhij,bhjv->bhiv', qk_c, v_new)
        s = s * gl_c[..., None, None] + jnp.einsum('bhik,bhiv->bhkv', kd_c, v_new)
        return s, o_c

    xs = tuple(jnp.moveaxis(arr, 2, 0) for arr in (u, w, qk, qg, kd, glast))
    s_fin, o = lax.scan(step, s0.astype(F32), xs)
    o = jnp.moveaxis(o, 0, 2)
    o = jnp.moveaxis(o, 1, 3).reshape(b, n * c, h, dv)[:, :t]
    return o, s_fin


def trunk_layer(x, pos, kv_bufs, conv_buf, s0, ffn_buf,
                norm_mix, w_in, conv_qkv, a_log, dt_bias, delta_norm,
                w_a_out, w_b_out, w_out, norm_ffn, w_gate, ffn_conv, w_up, w_down):
    b, t, _ = x.shape
    h = rmsnorm(x, norm_mix)
    proj = h @ w_in

    qkv_a = proj[..., :A_COLS].reshape(b, t, N_GROUPS, 3, A_HEADS, HEAD_DIM)
    q_a = partial_rope(qkv_a[:, :, :, 0], pos)
    k_a = partial_rope(qkv_a[:, :, :, 1], pos)
    v_a = qkv_a[:, :, :, 2]
    outs, lses, new_kv = [], [], []
    for gi, (window, dil) in enumerate(DIL_GROUPS):
        buf = kv_bufs[gi].astype(x.dtype)
        k_all = jnp.concatenate([buf[:, :, 0], k_a[:, :, gi]], axis=1)
        v_all = jnp.concatenate([buf[:, :, 1], v_a[:, :, gi]], axis=1)
        o_g, lse_g = dilated_group_attention(q_a[:, :, gi], k_all, v_all, buf.shape[1], window, dil)
        outs.append(o_g)
        lses.append(lse_g)
        keep = min(window, t)
        new_kv.append(jnp.stack([k_a[:, t - keep:, gi], v_a[:, t - keep:, gi]], axis=2))
    mix_w = jax.nn.softmax(jnp.stack(lses), axis=0)
    o_a = jnp.sum(mix_w[..., None] * jnp.stack(outs), axis=0).astype(x.dtype).reshape(b, t, A_WIDTH)

    qkv_b, conv_state = causal_depthwise_conv(conv_buf, proj[..., OFF_BQKV:OFF_BZ], conv_qkv)
    qkv_b = jax.nn.silu(qkv_b)
    q_b = l2norm(qkv_b[..., :B_QK_WIDTH].reshape(b, t, B_HEADS, B_DK))
    k_b = l2norm(qkv_b[..., B_QK_WIDTH:2 * B_QK_WIDTH].reshape(b, t, B_HEADS, B_DK))
    v_b = qkv_b[..., 2 * B_QK_WIDTH:].reshape(b, t, B_HEADS, B_DV)
    z_b = proj[..., OFF_BZ:OFF_BBETA].reshape(b, t, B_HEADS, B_DV)
    beta = jax.nn.sigmoid(proj[..., OFF_BBETA:OFF_BALPHA].astype(F32))
    log_decay = -jnp.exp(a_log.astype(F32)) * jax.nn.softplus(
        proj[..., OFF_BALPHA:OFF_GATE].astype(F32) + dt_bias.astype(F32))
    o_b, s_new = gated_delta_chunked(q_b, k_b, v_b, beta, log_decay, s0)
    o_b = rmsnorm(o_b, delta_norm) * jax.nn.silu(z_b.astype(F32))
    o_b = o_b.astype(x.dtype).reshape(b, t, B_V_WIDTH)

    gates = jax.nn.sigmoid(proj[..., OFF_GATE:])
    merged = gates[..., :D_MODEL] * (o_a @ w_a_out) + gates[..., D_MODEL:] * (o_b @ w_b_out)
    x = x + merged @ w_out

    h2 = rmsnorm(x, norm_ffn)
    g_ffn, ffn_state = causal_depthwise_conv(ffn_buf, h2 @ w_gate, ffn_conv)
    x = x + (jax.nn.silu(g_ffn) * (h2 @ w_up)) @ w_down
    return x, new_kv, conv_state, s_new.astype(s0.dtype), ffn_state


def setup_inputs(seed: int = 0) -> dict:
    key = jax.random.key(seed)
    ks = iter(jax.random.split(key, 40))

    def nrm(shape, scale=1.0):
        return jax.random.normal(next(ks), shape, F32) * scale

    def gain(shape):
        return 1.0 + 0.01 * nrm(shape)

    def buf_len(w):
        return min(w, PAST_LEN)

    a_log = jnp.log(jax.random.uniform(next(ks), (DEPTH, B_HEADS), F32, 1.0, 16.0))
    dt = jnp.exp(jax.random.uniform(next(ks), (DEPTH, B_HEADS), F32, math.log(1e-3), math.log(1e-1)))
    dt_bias = dt + jnp.log(-jnp.expm1(-dt))
    return {
        "x_prompt": nrm((BATCH, SEQ, D_MODEL)),
        "x_sample": nrm((DEC_BATCH, DEC_SEQ, D_MODEL)),
        "cache_kv_w128": nrm((DEPTH, DEC_BATCH, buf_len(DIL_GROUPS[0][0]), 2, A_HEADS, HEAD_DIM)),
        "cache_kv_w512": nrm((DEPTH, DEC_BATCH, buf_len(DIL_GROUPS[1][0]), 2, A_HEADS, HEAD_DIM)),
        "cache_kv_w2048": nrm((DEPTH, DEC_BATCH, buf_len(DIL_GROUPS[2][0]), 2, A_HEADS, HEAD_DIM)),
        "state_conv_qkv": nrm((DEPTH, DEC_BATCH, B_CONV - 1, B_CONV_COLS)),
        "state_delta": nrm((DEPTH, DEC_BATCH, B_HEADS, B_DK, B_DV), 0.5),
        "state_ffn_conv": nrm((DEPTH, DEC_BATCH, FFN_CONV - 1, D_FF)),
        "norm_mix": gain((DEPTH, D_MODEL)),
        "w_in": nrm((DEPTH, D_MODEL, IN_COLS), D_MODEL ** -0.5),
        "conv_qkv": nrm((DEPTH, B_CONV, B_CONV_COLS), B_CONV ** -0.5),
        "a_log": a_log,
        "dt_bias": dt_bias,
        "delta_norm": gain((DEPTH, B_DV)),
        "w_a_out": nrm((DEPTH, A_WIDTH, D_MODEL), A_WIDTH ** -0.5),
        "w_b_out": nrm((DEPTH, B_V_WIDTH, D_MODEL), B_V_WIDTH ** -0.5),
        "w_out": nrm((DEPTH, D_MODEL, D_MODEL), D_MODEL ** -0.5),
        "norm_ffn": gain((DEPTH, D_MODEL)),
        "w_gate": nrm((DEPTH, D_MODEL, D_FF), D_MODEL ** -0.5),
        "ffn_conv": nrm((DEPTH, FFN_CONV, D_FF), FFN_CONV ** -0.5),
        "w_up": nrm((DEPTH, D_MODEL, D_FF), D_MODEL ** -0.5),
        "w_down": nrm((DEPTH, D_FF, D_MODEL), D_FF ** -0.5),
        "norm_final": gain((D_MODEL,)),
    }


def reference(x_prompt, x_sample, cache_kv_w128, cache_kv_w512, cache_kv_w2048, state_conv_qkv, state_delta,
              state_ffn_conv, norm_mix, w_in, conv_qkv, a_log, dt_bias, delta_norm, w_a_out, w_b_out, w_out,
              norm_ffn, w_gate, ffn_conv, w_up, w_down, norm_final):
    bp, tp, _ = x_prompt.shape
    ts = x_sample.shape[1]
    dtype = x_prompt.dtype
    pos_p = jnp.arange(tp, dtype=jnp.int32)
    pos_s = PAST_LEN + jnp.arange(ts, dtype=jnp.int32)
    empty_kv = jnp.zeros((bp, 0, 2, A_HEADS, HEAD_DIM), dtype)
    conv0 = jnp.zeros((bp, B_CONV - 1, B_CONV_COLS), dtype)
    s0 = jnp.zeros((bp, B_HEADS, B_DK, B_DV), dtype)
    ffn0 = jnp.zeros((bp, FFN_CONV - 1, D_FF), dtype)
    hp, hs = x_prompt, x_sample
    p_new = [[] for _ in range(6)]
    s_new = [[] for _ in range(6)]
    for layer in range(DEPTH):
        weights = (norm_mix[layer], w_in[layer], conv_qkv[layer], a_log[layer], dt_bias[layer], delta_norm[layer],
                   w_a_out[layer], w_b_out[layer], w_out[layer], norm_ffn[layer], w_gate[layer], ffn_conv[layer],
                   w_up[layer], w_down[layer])
        hp, kv_p, conv_p, delta_p, ffn_p = trunk_layer(
            hp, pos_p, (empty_kv, empty_kv, empty_kv), conv0, s0, ffn0, *weights)
        hs, kv_s, conv_s, delta_s, ffn_s = trunk_layer(
            hs, pos_s, (cache_kv_w128[layer], cache_kv_w512[layer], cache_kv_w2048[layer]),
            state_conv_qkv[layer], state_delta[layer], state_ffn_conv[layer], *weights)
        for lst, val in zip(p_new, (kv_p[0], kv_p[1], kv_p[2], conv_p, delta_p, ffn_p)):
            lst.append(val)
        for lst, val in zip(s_new, (kv_s[0], kv_s[1], kv_s[2], conv_s, delta_s, ffn_s)):
            lst.append(val)
    y_prompt = rmsnorm(hp, norm_final)
    y_sample = rmsnorm(hs, norm_final)
    kv128_p, kv512_p, kv2048_p, conv_qkv_p, delta_p, ffn_conv_p = [jnp.stack(l) for l in p_new]
    kv128_s, kv512_s, kv2048_s, conv_qkv_s, delta_s, ffn_conv_s = [jnp.stack(l) for l in s_new]
    return (y_prompt, y_sample, kv128_p, kv512_p, kv2048_p, conv_qkv_p, delta_p, ffn_conv_p,
            kv128_s, kv512_s, kv2048_s, conv_qkv_s, delta_s, ffn_conv_s)
```

```python
import functools

import jax
import jax.numpy as jnp
from jax import lax
from jax.experimental import pallas as pl
from jax.experimental.pallas import tpu as pltpu

F32 = jnp.float32
BF16 = jnp.bfloat16
HIGHEST = lax.Precision.HIGHEST

HEAD_DIM = 128
ROT_DIM = HEAD_DIM // 4
ROT_HALF = ROT_DIM // 2
ROPE_THETA = 500000.0
DIL_GROUPS = ((128, 1), (512, 4), (2048, 16))
N_GROUPS = len(DIL_GROUPS)
A_HEADS = 4
A_WIDTH = A_HEADS * HEAD_DIM
A_COLS = N_GROUPS * 3 * A_WIDTH
KEYS_PER_QUERY = 128
B_HEADS = 16
B_DK = 128
B_DV = 128
B_WIDTH = B_HEADS * B_DK
B_CONV = 4
B_CONV_COLS = 3 * B_WIDTH
FFN_CONV = 3
EPS = 1e-6
PAST_LEN = 2048

LANES = 128
SUBLANES = 8
ROW_BLOCK = 128
SUPER = 2048
CHUNK = 128
HEAD_GROUP = 4
COL_BLOCK = HEAD_GROUP * LANES
VMEM_CAP_MB = 60


def _cparams(sem, vmem_mb):
    return pltpu.CompilerParams(dimension_semantics=sem,
                                vmem_limit_bytes=int(min(vmem_mb, VMEM_CAP_MB) * 2 ** 20))


def _bf(x):
    return x.astype(BF16)


def _dot(a, b):
    return jnp.dot(a, b, preferred_element_type=F32)


def _dot_nt(a, b):
    return lax.dot_general(a, b, (((1,), (1,)), ((), ())), preferred_element_type=F32)


def _dot_tn(a, b):
    return lax.dot_general(a, b, (((0,), (0,)), ((), ())), preferred_element_type=F32)


def _silu(x):
    return x * jax.nn.sigmoid(x)


def _norm_in_kernel(xp_ref, xs_ref, nw_ref, wba_ref, h_ref, xcat_ref, ba_ref, *, n_prompt_blocks):
    i = pl.program_id(0)

    def body(x):
        y = x * lax.rsqrt(jnp.mean(x * x, axis=-1, keepdims=True) + EPS) * nw_ref[...]
        h_ref[...] = _bf(y)
        xcat_ref[...] = x
        ba_ref[...] = jnp.dot(y, wba_ref[...], precision=HIGHEST, preferred_element_type=F32)

    @pl.when(i < n_prompt_blocks)
    def _():
        body(xp_ref[...])

    @pl.when(i >= n_prompt_blocks)
    def _():
        body(xs_ref[...])


def _norm_in(xp, xs, nw, wba):
    t, d = xp.shape
    b = xs.shape[0]
    npb, nsb = t // ROW_BLOCK, b // ROW_BLOCK
    m = t + b
    return pl.pallas_call(
        functools.partial(_norm_in_kernel, n_prompt_blocks=npb),
        grid=(npb + nsb,),
        in_specs=[
            pl.BlockSpec((ROW_BLOCK, d), lambda i: (jnp.minimum(i, npb - 1), 0)),
            pl.BlockSpec((ROW_BLOCK, d), lambda i: (jnp.maximum(i - npb, 0), 0)),
            pl.BlockSpec((1, d), lambda i: (0, 0)),
            pl.BlockSpec((d, LANES), lambda i: (0, 0)),
        ],
        out_specs=[
            pl.BlockSpec((ROW_BLOCK, d), lambda i: (i, 0)),
            pl.BlockSpec((ROW_BLOCK, d), lambda i: (i, 0)),
            pl.BlockSpec((ROW_BLOCK, LANES), lambda i: (i, 0)),
        ],
        out_shape=[
            jax.ShapeDtypeStruct((m, d), BF16),
            jax.ShapeDtypeStruct((m, d), F32),
            jax.ShapeDtypeStruct((m, LANES), F32),
        ],
        compiler_params=_cparams(("arbitrary",), 32),
        name="norm_in",
    )(xp, xs, nw, wba)


def _norm_mid_kernel(x_ref, nw_ref, h_ref):
    x = x_ref[...]
    y = x * lax.rsqrt(jnp.mean(x * x, axis=-1, keepdims=True) + EPS) * nw_ref[...]
    h_ref[...] = _bf(y)


def _norm_mid(x, nw):
    m, d = x.shape
    rb = 2 * ROW_BLOCK if m % (2 * ROW_BLOCK) == 0 else ROW_BLOCK
    return pl.pallas_call(
        _norm_mid_kernel,
        grid=(m // rb,),
        in_specs=[pl.BlockSpec((rb, d), lambda i: (i, 0)),
                  pl.BlockSpec((1, d), lambda i: (0, 0))],
        out_specs=pl.BlockSpec((rb, d), lambda i: (i, 0)),
        out_shape=jax.ShapeDtypeStruct((m, d), BF16),
        compiler_params=_cparams(("arbitrary",), 32),
        name="norm_mid",
    )(x, nw)


def _norm_out_kernel(x_ref, nw_ref, yp_ref, ys_ref, *, n_prompt_blocks):
    i = pl.program_id(0)
    x = x_ref[...]
    y = x * lax.rsqrt(jnp.mean(x * x, axis=-1, keepdims=True) + EPS) * nw_ref[...]

    @pl.when(i < n_prompt_blocks)
    def _():
        yp_ref[...] = y

    @pl.when(i >= n_prompt_blocks)
    def _():
        ys_ref[...] = y


def _norm_out(x, nw, t, b):
    d = x.shape[1]
    npb, nsb = t // ROW_BLOCK, b // ROW_BLOCK
    return pl.pallas_call(
        functools.partial(_norm_out_kernel, n_prompt_blocks=npb),
        grid=(npb + nsb,),
        in_specs=[pl.BlockSpec((ROW_BLOCK, d), lambda i: (i, 0)),
                  pl.BlockSpec((1, d), lambda i: (0, 0))],
        out_specs=[
            pl.BlockSpec((ROW_BLOCK, d), lambda i: (jnp.minimum(i, npb - 1), 0)),
            pl.BlockSpec((ROW_BLOCK, d), lambda i: (jnp.maximum(i - npb, 0), 0)),
        ],
        out_shape=[jax.ShapeDtypeStruct((t, d), F32), jax.ShapeDtypeStruct((b, d), F32)],
        compiler_params=_cparams(("arbitrary",), 32),
        name="norm_out",
    )(x, nw)


def _row_tile(m, cap):
    best = 16
    for c in range(16, cap + 1, 16):
        if m % c == 0:
            best = c
    return best


def _proj_kernel(h_ref, w_ref, cos_ref, sin_ref, o_ref, *, n_attn_tiles, gate_tile0):
    j = pl.program_id(1)
    acc = _dot(h_ref[...], w_ref[...])
    is_rope = jnp.logical_and(j < n_attn_tiles, j % 3 != 2)
    is_gate = j >= gate_tile0

    @pl.when(is_rope)
    def _():
        cos = cos_ref[...]
        sin = sin_ref[...]
        lane = lax.broadcasted_iota(jnp.int32, cos.shape, 1)
        for hh in range(A_HEADS):
            a = acc[:, hh * HEAD_DIM:(hh + 1) * HEAD_DIM]
            partner = jnp.where(lane < ROT_HALF,
                                pltpu.roll(a, HEAD_DIM - ROT_HALF, 1),
                                pltpu.roll(a, ROT_HALF, 1))
            o_ref[:, hh * HEAD_DIM:(hh + 1) * HEAD_DIM] = a * cos + partner * sin

    @pl.when(is_gate)
    def _():
        o_ref[...] = jax.nn.sigmoid(acc)

    @pl.when(jnp.logical_not(jnp.logical_or(is_rope, is_gate)))
    def _():
        o_ref[...] = acc


def _proj(h, w, cos, sin, gate_col0):
    m, k = h.shape
    n = w.shape[1]
    tn = A_WIDTH
    tm = _row_tile(m, 1040)
    return pl.pallas_call(
        functools.partial(_proj_kernel, n_attn_tiles=A_COLS // tn, gate_tile0=gate_col0 // tn),
        grid=(m // tm, n // tn),
        in_specs=[
            pl.BlockSpec((tm, k), lambda i, j: (i, 0)),
            pl.BlockSpec((k, tn), lambda i, j: (0, j)),
            pl.BlockSpec((tm, LANES), lambda i, j: (i, 0)),
            pl.BlockSpec((tm, LANES), lambda i, j: (i, 0)),
        ],
        out_specs=pl.BlockSpec((tm, tn), lambda i, j: (i, j)),
        out_shape=jax.ShapeDtypeStruct((m, n), F32),
        compiler_params=_cparams(("arbitrary", "arbitrary"), 48),
        name="proj_in",
    )(h, w, cos, sin)


def _merge_kernel(oa_ref, ob_ref, wa_ref, wb_ref, ga_ref, gb_ref, o_ref):
    a = _dot(oa_ref[...], wa_ref[...])
    b = _dot(ob_ref[...], wb_ref[...])
    o_ref[...] = _bf(ga_ref[...] * a + gb_ref[...] * b)


def _merge(oa, ob, wa, wb, proj, gate_col0):
    m = oa.shape[0]
    n = wa.shape[1]
    tn = COL_BLOCK
    tm = _row_tile(m, 1040)
    ga0 = gate_col0 // tn
    gb0 = (gate_col0 + n) // tn
    return pl.pallas_call(
        _merge_kernel,
        grid=(m // tm, n // tn),
        in_specs=[
            pl.BlockSpec((tm, oa.shape[1]), lambda i, j: (i, 0)),
            pl.BlockSpec((tm, ob.shape[1]), lambda i, j: (i, 0)),
            pl.BlockSpec((wa.shape[0], tn), lambda i, j: (0, j)),
            pl.BlockSpec((wb.shape[0], tn), lambda i, j: (0, j)),
            pl.BlockSpec((tm, tn), lambda i, j: (i, ga0 + j)),
            pl.BlockSpec((tm, tn), lambda i, j: (i, gb0 + j)),
        ],
        out_specs=pl.BlockSpec((tm, tn), lambda i, j: (i, j)),
        out_shape=jax.ShapeDtypeStruct((m, n), BF16),
        compiler_params=_cparams(("arbitrary", "arbitrary"), 40),
        name="merge",
    )(oa, ob, wa, wb, proj, proj)


def _mm_res_kernel(a_ref, w_ref, r_ref, o_ref):
    o_ref[...] = r_ref[...] + _dot(a_ref[...], w_ref[...])


def _mm_res(a, w, res, tm_cap, tn, name):
    m, k = a.shape
    n = w.shape[1]
    tm = _row_tile(m, tm_cap)
    return pl.pallas_call(
        _mm_res_kernel,
        grid=(m // tm, n // tn),
        in_specs=[
            pl.BlockSpec((tm, k), lambda i, j: (i, 0)),
            pl.BlockSpec((k, tn), lambda i, j: (0, j)),
            pl.BlockSpec((tm, tn), lambda i, j: (i, j)),
        ],
        out_specs=pl.BlockSpec((tm, tn), lambda i, j: (i, j)),
        out_shape=jax.ShapeDtypeStruct((m, n), F32),
        compiler_params=_cparams(("arbitrary", "arbitrary"), 56),
        name=name,
    )(a, w, res)


def _attn_prompt_kernel(*refs):
    grp = [refs[5 * g:5 * g + 5] for g in range(N_GROUPS)]
    o_ref, o_scr, l_scr = refs[5 * N_GROUPS:]
    first = pl.program_id(1) == 0
    row = lax.broadcasted_iota(jnp.int32, (LANES, LANES), 0)
    col = lax.broadcasted_iota(jnp.int32, (LANES, LANES), 1)
    mask_cur = col <= row
    mask_prev = col >= row
    mask_prev_first = col >= row + jnp.where(first, LANES, 0)
    scale = HEAD_DIM ** -0.5
    nq = KEYS_PER_QUERY

    for g, (_, dil) in enumerate(DIL_GROUPS):
        q_ref, k_ref, v_ref, kp_ref, vp_ref = grp[g]
        span = nq * dil

        def rows(ref, base, r, dil=dil):
            if dil == 1:
                return ref[pl.ds(base, nq), :]
            return ref[pl.ds(base + r, nq, stride=dil), :]

        for sb in range(SUPER // span):
            for r in range(dil):
                base = sb * span
                qt = _bf(rows(q_ref, base, r))
                kc = _bf(rows(k_ref, base, r))
                vc = _bf(rows(v_ref, base, r))
                if sb > 0:
                    kp = _bf(rows(k_ref, base - span, r))
                    vp = _bf(rows(v_ref, base - span, r))
                    mp = mask_prev
                else:
                    kp = _bf(rows(kp_ref, 0, r))
                    vp = _bf(rows(vp_ref, 0, r))
                    mp = mask_prev_first
                s_c = jnp.where(mask_cur, _dot_nt(qt, kc) * scale, -jnp.inf)
                s_p = jnp.where(mp, _dot_nt(qt, kp) * scale, -jnp.inf)
                mx = jnp.maximum(jnp.max(s_c, axis=-1, keepdims=True),
                                 jnp.max(s_p, axis=-1, keepdims=True))
                p_c = jnp.exp(s_c - mx)
                p_p = jnp.exp(s_p - mx)
                den = jnp.sum(p_c, axis=-1, keepdims=True) + jnp.sum(p_p, axis=-1, keepdims=True)
                o = (_dot(_bf(p_c), vc) + _dot(_bf(p_p), vp)) / den
                lse = jnp.broadcast_to(mx + jnp.log(den), (nq, LANES))
                if dil == 1:
                    o_scr[g, pl.ds(base, nq), :] = o
                    l_scr[g, pl.ds(base, nq), :] = lse
                else:
                    o_scr[g, pl.ds(base + r, nq, stride=dil), :] = o
                    l_scr[g, pl.ds(base + r, nq, stride=dil), :] = lse

    l0, l1, l2 = l_scr[0], l_scr[1], l_scr[2]
    mx = jnp.maximum(jnp.maximum(l0, l1), l2)
    e0, e1, e2 = jnp.exp(l0 - mx), jnp.exp(l1 - mx), jnp.exp(l2 - mx)
    o = (e0 * o_scr[0] + e1 * o_scr[1] + e2 * o_scr[2]) / (e0 + e1 + e2)
    o_ref[...] = _bf(o)


def _attn_prompt(proj, t):
    nsup = t // SUPER
    in_specs = []
    args = []
    for g, (_, dil) in enumerate(DIL_GROUPS):
        span = KEYS_PER_QUERY * dil
        per = SUPER // span
        cb = g * 3 * A_HEADS
        for part in range(3):
            in_specs.append(pl.BlockSpec(
                (SUPER, HEAD_DIM), lambda h, i, c=cb + part * A_HEADS: (i, c + h)))
            args.append(proj)
        for part in (1, 2):
            in_specs.append(pl.BlockSpec(
                (span, HEAD_DIM),
                lambda h, i, c=cb + part * A_HEADS, per=per: (jnp.maximum(i * per - 1, 0), c + h)))
            args.append(proj)
    return pl.pallas_call(
        _attn_prompt_kernel,
        grid=(A_HEADS, nsup),
        in_specs=in_specs,
        out_specs=pl.BlockSpec((SUPER, HEAD_DIM), lambda h, i: (i, h)),
        out_shape=jax.ShapeDtypeStruct((t, A_WIDTH), BF16),
        scratch_shapes=[pltpu.VMEM((N_GROUPS, SUPER, HEAD_DIM), F32),
                        pltpu.VMEM((N_GROUPS, SUPER, HEAD_DIM), F32)],
        compiler_params=_cparams(("arbitrary", "arbitrary"), 48),
        name="attn_prompt",
    )(*args)


def _attn_sample_kernel(qkv_ref, c0_ref, c1_ref, c2_ref, o_ref, *, bb):
    caches = (c0_ref, c1_ref, c2_ref)
    scale = HEAD_DIM ** -0.5
    for b in range(bb):
        for hh in range(A_HEADS):
            outs, lses = [], []
            for g in range(N_GROUPS):
                c0 = g * 3 * A_WIDTH + hh * HEAD_DIM
                q = qkv_ref[b:b + 1, c0:c0 + HEAD_DIM]
                kn = qkv_ref[b:b + 1, c0 + A_WIDTH:c0 + A_WIDTH + HEAD_DIM]
                vn = qkv_ref[b:b + 1, c0 + 2 * A_WIDTH:c0 + 2 * A_WIDTH + HEAD_DIM]
                kc = caches[g][b, :, hh * HEAD_DIM:(hh + 1) * HEAD_DIM]
                vc = caches[g][b, :, A_WIDTH + hh * HEAD_DIM:A_WIDTH + (hh + 1) * HEAD_DIM]
                s = jnp.sum(kc * q, axis=-1, keepdims=True) * scale
                sn = jnp.sum(kn * q, axis=-1, keepdims=True) * scale
                mx = jnp.maximum(jnp.max(s, axis=0, keepdims=True), sn)
                p = jnp.exp(s - mx)
                pn = jnp.exp(sn - mx)
                den = jnp.sum(p, axis=0, keepdims=True) + pn
                outs.append((jnp.sum(p * vc, axis=0, keepdims=True) + pn * vn) / den)
                lses.append(mx + jnp.log(den))
            mx = jnp.maximum(jnp.maximum(lses[0], lses[1]), lses[2])
            es = [jnp.exp(l - mx) for l in lses]
            o = (es[0] * outs[0] + es[1] * outs[1] + es[2] * outs[2]) / (es[0] + es[1] + es[2])
            o_ref[b:b + 1, hh * HEAD_DIM:(hh + 1) * HEAD_DIM] = o


def _attn_sample(proj, caches, t, b):
    bb = SUBLANES
    kv_w = 2 * A_WIDTH
    in_specs = [pl.BlockSpec((bb, A_COLS), lambda i: (t // bb + i, 0))]
    args = [proj]
    for (win, dil), c in zip(DIL_GROUPS, caches):
        args.append(c.reshape(b, win // dil, dil * kv_w))
        in_specs.append(pl.BlockSpec((bb, win // dil, kv_w), lambda i: (i, 0, 0)))
    return pl.pallas_call(
        functools.partial(_attn_sample_kernel, bb=bb),
        grid=(b // bb,),
        in_specs=in_specs,
        out_specs=pl.BlockSpec((bb, A_WIDTH), lambda i: (i, 0)),
        out_shape=jax.ShapeDtypeStruct((b, A_WIDTH), F32),
        compiler_params=_cparams(("arbitrary",), 40),
        name="attn_sample",
    )(*args)


def _beta_decay_kernel(ba_ref, alog_ref, dtb_ref, ltri_ref, e_ref, beta_ref, gc_ref, *, cumulative):
    x = ba_ref[...]
    lane = lax.broadcasted_iota(jnp.int32, x.shape, 1)
    z = x + dtb_ref[...]
    softplus = jnp.maximum(z, 0.0) + jnp.log1p(jnp.exp(-jnp.abs(z)))
    g = -jnp.exp(alog_ref[...]) * softplus
    if cumulative:
        g = jnp.dot(ltri_ref[...], g, precision=HIGHEST, preferred_element_type=F32)
    vals = jnp.where(lane < B_HEADS, jax.nn.sigmoid(x), g)
    wide = jnp.dot(vals, e_ref[...], precision=HIGHEST, preferred_element_type=F32)
    beta_ref[...] = wide[:, :B_WIDTH]
    gc_ref[...] = wide[:, B_WIDTH:]


def _beta_decay(ba, alog_row, dtb_row, ltri, expand, row0, rows, cumulative):
    blk0 = row0 // CHUNK
    return pl.pallas_call(
        functools.partial(_beta_decay_kernel, cumulative=cumulative),
        grid=(rows // CHUNK,),
        in_specs=[
            pl.BlockSpec((CHUNK, LANES), lambda i: (blk0 + i, 0)),
            pl.BlockSpec((1, LANES), lambda i: (0, 0)),
            pl.BlockSpec((1, LANES), lambda i: (0, 0)),
            pl.BlockSpec((CHUNK, CHUNK), lambda i: (0, 0)),
            pl.BlockSpec((LANES, 2 * B_WIDTH), lambda i: (0, 0)),
        ],
        out_specs=[pl.BlockSpec((CHUNK, B_WIDTH), lambda i: (i, 0)),
                   pl.BlockSpec((CHUNK, B_WIDTH), lambda i: (i, 0))],
        out_shape=[jax.ShapeDtypeStruct((rows, B_WIDTH), F32),
                   jax.ShapeDtypeStruct((rows, B_WIDTH), F32)],
        compiler_params=_cparams(("arbitrary",), 32),
        name="beta_decay",
    )(ba, alog_row, dtb_row, ltri, expand)


def _l2norm_heads(x):
    parts = []
    for hh in range(x.shape[1] // HEAD_DIM):
        a = x[:, hh * HEAD_DIM:(hh + 1) * HEAD_DIM]
        parts.append(a * lax.rsqrt(jnp.sum(a * a, axis=-1, keepdims=True) + EPS))
    return jnp.concatenate(parts, axis=1)


def _conv_prompt_kernel(x_ref, cw_ref, beta_ref, o_ref, ob_ref, buf, *, rb):
    c = pl.program_id(0)
    i = pl.program_id(1)
    halo = SUBLANES

    @pl.when(i == 0)
    def _():
        buf[0:halo, :] = jnp.zeros((halo, buf.shape[1]), F32)

    buf[halo:halo + rb, :] = x_ref[...]
    acc = buf[pl.ds(halo, rb), :] * cw_ref[B_CONV - 1:B_CONV, :]
    for s in range(1, B_CONV):
        acc = acc + buf[pl.ds(halo - s, rb), :] * cw_ref[B_CONV - 1 - s:B_CONV - s, :]
    buf[0:halo, :] = buf[rb:rb + halo, :]
    y = _silu(acc)
    n_qk = 2 * B_HEADS // HEAD_GROUP
    beta = beta_ref[...]

    @pl.when(c < n_qk // 2)
    def _():
        o_ref[...] = _l2norm_heads(y) * (B_DK ** -0.5)
        ob_ref[...] = jnp.zeros_like(y)

    @pl.when(jnp.logical_and(c >= n_qk // 2, c < n_qk))
    def _():
        kn = _l2norm_heads(y)
        o_ref[...] = kn
        ob_ref[...] = kn * beta

    @pl.when(c >= n_qk)
    def _():
        o_ref[...] = y
        ob_ref[...] = y * beta


def _conv_prompt(proj, cw, beta, t, col0):
    rb = 256
    ncb = B_CONV_COLS // COL_BLOCK
    cb0 = col0 // COL_BLOCK
    hgs = B_HEADS // HEAD_GROUP
    return pl.pallas_call(
        functools.partial(_conv_prompt_kernel, rb=rb),
        grid=(ncb, t // rb),
        in_specs=[
            pl.BlockSpec((rb, COL_BLOCK), lambda c, i: (i, cb0 + c)),
            pl.BlockSpec((B_CONV, COL_BLOCK), lambda c, i: (0, c)),
            pl.BlockSpec((rb, COL_BLOCK), lambda c, i: (i, c % hgs)),
        ],
        out_specs=[pl.BlockSpec((rb, COL_BLOCK), lambda c, i: (i, c)),
                   pl.BlockSpec((rb, COL_BLOCK), lambda c, i: (i, c))],
        out_shape=[jax.ShapeDtypeStruct((t, B_CONV_COLS), F32),
                   jax.ShapeDtypeStruct((t, B_CONV_COLS), F32)],
        scratch_shapes=[pltpu.VMEM((rb + SUBLANES, COL_BLOCK), F32)],
        compiler_params=_cparams(("arbitrary", "arbitrary"), 32),
        name="conv_prompt",
    )(proj, cw, beta)


def _gated_norm(o, z, nw):
    y = o * lax.rsqrt(jnp.mean(o * o, axis=-1, keepdims=True) + EPS) * nw
    return y * _silu(z)


INV_BASE = 16


def _unit_lower_inverse(a, eye, row, col):
    c = a.shape[0]
    def blk(size):
        sh = size.bit_length() - 1
        return (row >> sh) == (col >> sh)

    d = jnp.where(blk(INV_BASE), a, 0.0)
    tm = eye - d
    pw = d
    for _ in range(INV_BASE.bit_length() - 2):
        pw = _dot_inv(pw, pw)
        tm = tm + _dot_inv(tm, pw)
    size = INV_BASE
    while size < c:
        off = jnp.where(jnp.logical_and(blk(2 * size), jnp.logical_not(blk(size))), a, 0.0)
        tm = tm - _dot_inv(tm, _dot_inv(off, tm))
        size *= 2
    return tm


def _dot_inv(a, b):
    return _dot(_bf(a), _bf(b))


def _delta_prompt_kernel(q_ref, k_ref, kb_ref, vb_ref, gc_ref, z_ref, nw_ref, o_ref, sfin_ref, s_scr):
    j = pl.program_id(1)
    c = CHUNK

    @pl.when(j == 0)
    def _():
        s_scr[...] = jnp.zeros_like(s_scr)

    row = lax.broadcasted_iota(jnp.int32, (c, c), 0)
    col = lax.broadcasted_iota(jnp.int32, (c, c), 1)
    tril = col <= row
    strict = col < row
    eye = (col == row).astype(F32)

    for hh in range(HEAD_GROUP):
        sl = slice(hh * LANES, (hh + 1) * LANES)
        q, k, kb, vb, gc = q_ref[:, sl], k_ref[:, sl], kb_ref[:, sl], vb_ref[:, sl], gc_ref[:, sl]
        decay = jnp.exp(jnp.where(tril, gc - gc.T, -jnp.inf))
        e_g = jnp.exp(gc)
        g_last = gc[c - 1:c, :]
        k_bf = _bf(k)
        a = jnp.where(strict, _dot_nt(_bf(kb), k_bf) * decay, 0.0)
        qk = _dot_nt(_bf(q), k_bf) * decay
        tm = _unit_lower_inverse(a, eye, row, col)
        tm_bf = _bf(tm)
        u = _dot(tm_bf, _bf(vb))
        w = _dot(tm_bf, _bf(kb * e_g))
        s = s_scr[hh]
        s_bf = _bf(s)
        v_new = u - _dot(_bf(w), s_bf)
        v_bf = _bf(v_new)
        o = _dot(_bf(q * e_g), s_bf) + _dot(_bf(qk), v_bf)
        kd = k * jnp.exp(g_last - gc)
        s_scr[hh] = s * jnp.exp(g_last) + _dot_tn(_bf(kd), v_bf)
        o_ref[:, sl] = _gated_norm(o, z_ref[:, sl], nw_ref[...]).astype(BF16)

    @pl.when(j == pl.num_programs(1) - 1)
    def _():
        sfin_ref[...] = s_scr[...]


def _delta_prompt(qkv, qkvb, gc, proj, nw, t, z_col0):
    hgs = B_HEADS // HEAD_GROUP
    zb0 = z_col0 // COL_BLOCK
    blk = lambda off: pl.BlockSpec((CHUNK, COL_BLOCK), lambda h, j, off=off: (j, off + h))
    return pl.pallas_call(
        _delta_prompt_kernel,
        grid=(hgs, t // CHUNK),
        in_specs=[blk(0), blk(hgs), blk(hgs), blk(2 * hgs), blk(0), blk(zb0),
                  pl.BlockSpec((1, LANES), lambda h, j: (0, 0))],
        out_specs=[pl.BlockSpec((CHUNK, COL_BLOCK), lambda h, j: (j, h)),
                   pl.BlockSpec((HEAD_GROUP, B_DK, B_DV), lambda h, j: (h, 0, 0))],
        out_shape=[jax.ShapeDtypeStruct((t, B_WIDTH), BF16),
                   jax.ShapeDtypeStruct((B_HEADS, B_DK, B_DV), F32)],
        scratch_shapes=[pltpu.VMEM((HEAD_GROUP, B_DK, B_DV), F32)],
        compiler_params=_cparams(("arbitrary", "arbitrary"), 32),
        name="delta_prompt",
    )(qkv, qkv, qkvb, qkvb, gc, proj, nw)


def _conv_sample_kernel(x_ref, s0_ref, s1_ref, s2_ref, cw_ref, o_ref):
    c = pl.program_id(0)
    acc = (s0_ref[...] * cw_ref[0:1, :] + s1_ref[...] * cw_ref[1:2, :]
           + s2_ref[...] * cw_ref[2:3, :] + x_ref[...] * cw_ref[3:4, :])
    y = _silu(acc)
    n_qk = 2 * B_HEADS // HEAD_GROUP

    @pl.when(c < n_qk // 2)
    def _():
        o_ref[...] = _l2norm_heads(y) * (B_DK ** -0.5)

    @pl.when(jnp.logical_and(c >= n_qk // 2, c < n_qk))
    def _():
        o_ref[...] = _l2norm_heads(y)

    @pl.when(c >= n_qk)
    def _():
        o_ref[...] = y


def _conv_sample(proj, state, cw, t, b, col0):
    ncb = B_CONV_COLS // COL_BLOCK
    cb0 = col0 // COL_BLOCK
    rb0 = t // b
    sspec = pl.BlockSpec((b, COL_BLOCK), lambda c: (0, c))
    return pl.pallas_call(
        _conv_sample_kernel,
        grid=(ncb,),
        in_specs=[pl.BlockSpec((b, COL_BLOCK), lambda c: (rb0, cb0 + c)), sspec, sspec, sspec,
                  pl.BlockSpec((B_CONV, COL_BLOCK), lambda c: (0, c))],
        out_specs=pl.BlockSpec((b, COL_BLOCK), lambda c: (0, c)),
        out_shape=jax.ShapeDtypeStruct((b, B_CONV_COLS), F32),
        compiler_params=_cparams(("arbitrary",), 32),
        name="conv_sample",
    )(proj, state[:, 0], state[:, 1], state[:, 2], cw)


def _delta_sample_kernel(q_ref, k_ref, v_ref, beta_ref, g_ref, z_ref, nw_ref, s_ref, o_ref, snew_ref, *, bb):
    row = lax.broadcasted_iota(jnp.int32, (LANES, LANES), 0)
    col = lax.broadcasted_iota(jnp.int32, (LANES, LANES), 1)
    eye = col == row

    def column(r):
        return jnp.sum(jnp.where(eye, r, 0.0), axis=-1, keepdims=True)

    for b in range(bb):
        for hh in range(HEAD_GROUP):
            sl = slice(hh * LANES, (hh + 1) * LANES)
            q, k, v = q_ref[b:b + 1, sl], k_ref[b:b + 1, sl], v_ref[b:b + 1, sl]
            beta = beta_ref[b:b + 1, sl]
            e_g = jnp.exp(g_ref[b:b + 1, sl])
            s = s_ref[b, hh]
            k_col = column(k)
            ks = jnp.sum(s * k_col, axis=0, keepdims=True)
            qs = jnp.sum(s * column(q), axis=0, keepdims=True)
            v_new = beta * (v - e_g * ks)
            qk = jnp.sum(q * k, axis=-1, keepdims=True)
            o = e_g * qs + qk * v_new
            snew_ref[b, hh] = s * e_g + k_col * v_new
            o_ref[b:b + 1, sl] = _gated_norm(o, z_ref[b:b + 1, sl], nw_ref[...])


def _delta_sample(qkv, beta, g, proj, nw, state, t, b, z_col0):
    bb = SUBLANES
    hgs = B_HEADS // HEAD_GROUP
    zb0 = z_col0 // COL_BLOCK
    rb0 = t // bb
    blk = lambda off: pl.BlockSpec((bb, COL_BLOCK), lambda i, h, off=off: (i, off + h))
    sspec = pl.BlockSpec((bb, HEAD_GROUP, B_DK, B_DV), lambda i, h: (i, h, 0, 0))
    return pl.pallas_call(
        functools.partial(_delta_sample_kernel, bb=bb),
        grid=(b // bb, hgs),
        in_specs=[blk(0), blk(hgs), blk(2 * hgs), blk(0), blk(0),
                  pl.BlockSpec((bb, COL_BLOCK), lambda i, h: (rb0 + i, zb0 + h)),
                  pl.BlockSpec((1, LANES), lambda i, h: (0, 0)), sspec],
        out_specs=[pl.BlockSpec((bb, COL_BLOCK), lambda i, h: (i, h)), sspec],
        out_shape=[jax.ShapeDtypeStruct((b, B_WIDTH), F32),
                   jax.ShapeDtypeStruct(state.shape, F32)],
        compiler_params=_cparams(("arbitrary", "arbitrary"), 32),
        name="delta_sample",
    )(qkv, qkv, qkv, beta, g, proj, nw, state)


def _ffn_prompt_kernel(h_ref, wg_ref, wu_ref, cw_ref, act_ref, tail_ref, buf, *, tm):
    i = pl.program_id(1)
    halo = SUBLANES

    @pl.when(i == 0)
    def _():
        buf[0:halo, :] = jnp.zeros((halo, buf.shape[1]), F32)

    h = h_ref[...]
    buf[halo:halo + tm, :] = _dot(h, wg_ref[...])
    acc = buf[pl.ds(halo, tm), :] * cw_ref[FFN_CONV - 1:FFN_CONV, :]
    for s in range(1, FFN_CONV):
        acc = acc + buf[pl.ds(halo - s, tm), :] * cw_ref[FFN_CONV - 1 - s:FFN_CONV - s, :]
    tail = buf[tm:tm + halo, :]
    buf[0:halo, :] = tail
    tail_ref[...] = tail
    act_ref[...] = _bf(_silu(acc) * _dot(h, wu_ref[...]))


def _ffn_prompt(h2, wg, wu, cw, t):
    m, d = h2.shape
    f = wg.shape[1]
    tf = 256
    tm = _row_tile(t, 1024)
    return pl.pallas_call(
        functools.partial(_ffn_prompt_kernel, tm=tm),
        grid=(f // tf, t // tm),
        in_specs=[
            pl.BlockSpec((tm, d), lambda c, i: (i, 0)),
            pl.BlockSpec((d, tf), lambda c, i: (0, c)),
            pl.BlockSpec((d, tf), lambda c, i: (0, c)),
            pl.BlockSpec((FFN_CONV, tf), lambda c, i: (0, c)),
        ],
        out_specs=[pl.BlockSpec((tm, tf), lambda c, i: (i, c)),
                   pl.BlockSpec((SUBLANES, tf), lambda c, i: (0, c))],
        out_shape=[jax.ShapeDtypeStruct((m, f), BF16),
                   jax.ShapeDtypeStruct((SUBLANES, f), F32)],
        scratch_shapes=[pltpu.VMEM((tm + SUBLANES, tf), F32)],
        compiler_params=_cparams(("arbitrary", "arbitrary"), 48),
        name="ffn_prompt",
    )(h2, wg, wu, cw)


def _ffn_sample_kernel(h_ref, wg_ref, wu_ref, cw_ref, s0_ref, s1_ref, act_in_ref, act_ref, g_ref):
    del act_in_ref
    h = h_ref[...]
    g = _dot(h, wg_ref[...])
    g_ref[...] = g
    acc = s0_ref[...] * cw_ref[0:1, :] + s1_ref[...] * cw_ref[1:2, :] + g * cw_ref[2:3, :]
    act_ref[...] = _bf(_silu(acc) * _dot(h, wu_ref[...]))


def _ffn_sample(h2, wg, wu, cw, state, act, t, b):
    d = h2.shape[1]
    f = wg.shape[1]
    tf = 256
    rb0 = t // b
    sspec = pl.BlockSpec((b, tf), lambda c: (0, c))
    return pl.pallas_call(
        _ffn_sample_kernel,
        grid=(f // tf,),
        in_specs=[
            pl.BlockSpec((b, d), lambda c: (rb0, 0)),
            pl.BlockSpec((d, tf), lambda c: (0, c)),
            pl.BlockSpec((d, tf), lambda c: (0, c)),
            pl.BlockSpec((FFN_CONV, tf), lambda c: (0, c)),
            sspec, sspec,
            pl.BlockSpec(memory_space=pl.ANY),
        ],
        out_specs=[pl.BlockSpec((b, tf), lambda c: (rb0, c)),
                   pl.BlockSpec((b, tf), lambda c: (0, c))],
        out_shape=[jax.ShapeDtypeStruct(act.shape, BF16),
                   jax.ShapeDtypeStruct((b, f), F32)],
        input_output_aliases={6: 0},
        compiler_params=_cparams(("arbitrary",), 32),
        name="ffn_sample",
    )(h2, wg, wu, cw, state[:, 0], state[:, 1], act)


def _rope_tables(t, b):
    inv = ROPE_THETA ** (-jnp.arange(0, ROT_DIM, 2, dtype=F32) / ROT_DIM)
    pos = jnp.concatenate([jnp.arange(t, dtype=jnp.int32),
                           jnp.full((b,), PAST_LEN, jnp.int32)]).astype(F32)
    ang = pos[:, None] * inv[None, :]
    cos, sin = jnp.cos(ang), jnp.sin(ang)
    m = t + b
    pad = LANES - ROT_DIM
    cos_t = jnp.concatenate([cos, cos, jnp.ones((m, pad), F32)], axis=1)
    sin_t = jnp.concatenate([-sin, sin, jnp.zeros((m, pad), F32)], axis=1)
    return cos_t, sin_t


def kernel(x_prompt, x_sample, cache_kv_w128, cache_kv_w512, cache_kv_w2048, state_conv_qkv, state_delta,
           state_ffn_conv, norm_mix, w_in, conv_qkv, a_log, dt_bias, delta_norm, w_a_out, w_b_out, w_out,
           norm_ffn, w_gate, ffn_conv, w_up, w_down, norm_final):
    assert norm_mix.shape[0] == 1, "single layer only"
    assert x_prompt.shape[0] == 1 and x_sample.shape[1] == 1
    t, d = x_prompt.shape[1], x_prompt.shape[2]
    b = x_sample.shape[0]
    assert t % SUPER == 0 and b == ROW_BLOCK
    caches = (cache_kv_w128[0], cache_kv_w512[0], cache_kv_w2048[0])
    for (win, _), c in zip(DIL_GROUPS, caches):
        assert c.shape[1] == win, "cache must hold a full window"

    off_bqkv = A_COLS
    off_bz = off_bqkv + B_CONV_COLS
    off_ba = off_bz + B_WIDTH
    off_gate = off_ba + 2 * B_HEADS
    w_in0 = w_in[0]
    assert w_in0.shape[1] == off_gate + 2 * d
    w_main = jnp.concatenate([_bf(w_in0[:, :off_ba]), _bf(w_in0[:, off_gate:])], axis=1)
    w_ba = jnp.pad(w_in0[:, off_ba:off_gate], ((0, 0), (0, LANES - 2 * B_HEADS)))
    gate_col0 = off_ba

    xp = x_prompt.reshape(t, d)
    xs = x_sample.reshape(b, d)
    h, x_cat, ba = _norm_in(xp, xs, norm_mix, w_ba)
    cos_t, sin_t = _rope_tables(t, b)
    proj = _proj(h, w_main, cos_t, sin_t, gate_col0)

    oa_p = _attn_prompt(proj, t)
    oa_s = _attn_sample(proj, caches, t, b)
    o_a = jnp.concatenate([oa_p, _bf(oa_s)], axis=0)

    pad_heads = (0, LANES - 2 * B_HEADS)
    alog_row = jnp.pad(a_log[0], (B_HEADS, LANES - 2 * B_HEADS)).reshape(1, LANES)
    dtb_row = jnp.pad(dt_bias[0], (B_HEADS, LANES - 2 * B_HEADS)).reshape(1, LANES)
    del pad_heads
    ltri = jnp.tril(jnp.ones((CHUNK, CHUNK), F32))
    head_of_col = jnp.arange(2 * B_WIDTH, dtype=jnp.int32) // LANES
    expand = (jnp.arange(LANES, dtype=jnp.int32)[:, None] == head_of_col[None, :]).astype(F32)
    beta_p, gc_p = _beta_decay(ba, alog_row, dtb_row, ltri, expand, 0, t, True)
    beta_s, g_s = _beta_decay(ba, alog_row, dtb_row, ltri, expand, t, b, False)
    cw_qkv = conv_qkv[0]
    qkv_p, qkvb_p = _conv_prompt(proj, cw_qkv, beta_p, t, off_bqkv)
    ob_p, delta_p = _delta_prompt(qkv_p, qkvb_p, gc_p, proj, delta_norm, t, off_bz)
    qkv_s = _conv_sample(proj, state_conv_qkv[0], cw_qkv, t, b, off_bqkv)
    ob_s, delta_s = _delta_sample(qkv_s, beta_s, g_s, proj, delta_norm, state_delta[0], t, b, off_bz)
    o_b = jnp.concatenate([ob_p, _bf(ob_s)], axis=0)

    merged = _merge(o_a, o_b, _bf(w_a_out[0]), _bf(w_b_out[0]), proj, gate_col0)
    x1 = _mm_res(merged, _bf(w_out[0]), x_cat, 1040, 512, "out_proj")

    h2 = _norm_mid(x1, norm_ffn)
    wg, wu = _bf(w_gate[0]), _bf(w_up[0])
    act, g_tail = _ffn_prompt(h2, wg, wu, ffn_conv[0], t)
    act, g_smp = _ffn_sample(h2, wg, wu, ffn_conv[0], state_ffn_conv[0], act, t, b)
    x2 = _mm_res(act, _bf(w_down[0]), x1, 640, 256, "down_proj")
    y_p, y_s = _norm_out(x2, norm_final.reshape(1, d), t, b)

    kv_p, kv_s = [], []
    for g, (win, _) in enumerate(DIL_GROUPS):
        c0 = g * 3 * A_WIDTH + A_WIDTH
        keep = min(win, t)
        kv_p.append(proj[t - keep:t, c0:c0 + 2 * A_WIDTH].reshape(1, 1, keep, 2, A_HEADS, HEAD_DIM))
        kv_s.append(proj[t:, c0:c0 + 2 * A_WIDTH].reshape(1, b, 1, 2, A_HEADS, HEAD_DIM))
    conv_p = proj[t - (B_CONV - 1):t, off_bqkv:off_bz].reshape(1, 1, B_CONV - 1, B_CONV_COLS)
    conv_s = jnp.concatenate([state_conv_qkv[0][:, 1:], proj[t:, off_bqkv:off_bz][:, None]], axis=1)[None]
    ffn_p = g_tail[SUBLANES - (FFN_CONV - 1):].reshape(1, 1, FFN_CONV - 1, -1)
    ffn_s = jnp.concatenate([state_ffn_conv[0][:, 1:], g_smp[:, None]], axis=1)[None]
    return (y_p.reshape(1, t, d), y_s.reshape(b, 1, d), kv_p[0], kv_p[1], kv_p[2],
            conv_p, delta_p.reshape(1, 1, B_HEADS, B_DK, B_DV), ffn_p,
            kv_s[0], kv_s[1], kv_s[2], conv_s, delta_s[None], ffn_s)
```

```python
import functools

import jax
import jax.numpy as jnp
from jax import lax
from jax.experimental import pallas as pl
from jax.experimental.pallas import tpu as pltpu

F32 = jnp.float32
BF16 = jnp.bfloat16
HIGHEST = lax.Precision.HIGHEST

HEAD_DIM = 128
ROT_DIM = HEAD_DIM // 4
ROT_HALF = ROT_DIM // 2
ROPE_THETA = 500000.0
DIL_GROUPS = ((128, 1), (512, 4), (2048, 16))
N_GROUPS = len(DIL_GROUPS)
A_HEADS = 4
A_WIDTH = A_HEADS * HEAD_DIM
A_COLS = N_GROUPS * 3 * A_WIDTH
KEYS_PER_QUERY = 128
B_HEADS = 16
B_DK = 128
B_DV = 128
B_WIDTH = B_HEADS * B_DK
B_CONV = 4
B_CONV_COLS = 3 * B_WIDTH
FFN_CONV = 3
EPS = 1e-6
PAST_LEN = 2048

LANES = 128
SUBLANES = 8
ROW_BLOCK = 128
SUPER = 2048
CHUNK = 128
HEAD_GROUP = 4
COL_BLOCK = HEAD_GROUP * LANES
VMEM_CAP_MB = 60


def _cparams(sem, vmem_mb):
    return pltpu.CompilerParams(dimension_semantics=sem,
                                vmem_limit_bytes=int(min(vmem_mb, VMEM_CAP_MB) * 2 ** 20))


def _bf(x):
    return x.astype(BF16)


def _dot(a, b):
    return jnp.dot(a, b, preferred_element_type=F32)


def _dot_nt(a, b):
    return lax.dot_general(a, b, (((1,), (1,)), ((), ())), preferred_element_type=F32)


def _dot_tn(a, b):
    return lax.dot_general(a, b, (((0,), (0,)), ((), ())), preferred_element_type=F32)


def _silu(x):
    return x * jax.nn.sigmoid(x)


def _norm_in_kernel(xp_ref, xs_ref, nw_ref, wba_ref, h_ref, xcat_ref, ba_ref, *, n_prompt_blocks):
    i = pl.program_id(0)

    def body(x):
        y = x * lax.rsqrt(jnp.mean(x * x, axis=-1, keepdims=True) + EPS) * nw_ref[...]
        h_ref[...] = _bf(y)
        xcat_ref[...] = x
        ba_ref[...] = lax.dot_general(y, wba_ref[...], (((1,), (1,)), ((), ())),
                                      precision=HIGHEST, preferred_element_type=F32)

    @pl.when(i < n_prompt_blocks)
    def _():
        body(xp_ref[...])

    @pl.when(i >= n_prompt_blocks)
    def _():
        body(xs_ref[...])


def _norm_in(xp, xs, nw, wba):
    t, d = xp.shape
    b = xs.shape[0]
    npb, nsb = t // ROW_BLOCK, b // ROW_BLOCK
    m = t + b
    return pl.pallas_call(
        functools.partial(_norm_in_kernel, n_prompt_blocks=npb),
        grid=(npb + nsb,),
        in_specs=[
            pl.BlockSpec((ROW_BLOCK, d), lambda i: (jnp.minimum(i, npb - 1), 0)),
            pl.BlockSpec((ROW_BLOCK, d), lambda i: (jnp.maximum(i - npb, 0), 0)),
            pl.BlockSpec((1, d), lambda i: (0, 0)),
            pl.BlockSpec((LANES, d), lambda i: (0, 0)),
        ],
        out_specs=[
            pl.BlockSpec((ROW_BLOCK, d), lambda i: (i, 0)),
            pl.BlockSpec((ROW_BLOCK, d), lambda i: (i, 0)),
            pl.BlockSpec((ROW_BLOCK, LANES), lambda i: (i, 0)),
        ],
        out_shape=[
            jax.ShapeDtypeStruct((m, d), BF16),
            jax.ShapeDtypeStruct((m, d), F32),
            jax.ShapeDtypeStruct((m, LANES), F32),
        ],
        compiler_params=_cparams(("arbitrary",), 32),
        name="norm_in",
    )(xp, xs, nw, wba)


def _norm_mid_kernel(x_ref, nw_ref, h_ref):
    x = x_ref[...]
    y = x * lax.rsqrt(jnp.mean(x * x, axis=-1, keepdims=True) + EPS) * nw_ref[...]
    h_ref[...] = _bf(y)


def _norm_mid(x, nw):
    m, d = x.shape
    rb = 2 * ROW_BLOCK if m % (2 * ROW_BLOCK) == 0 else ROW_BLOCK
    return pl.pallas_call(
        _norm_mid_kernel,
        grid=(m // rb,),
        in_specs=[pl.BlockSpec((rb, d), lambda i: (i, 0)),
                  pl.BlockSpec((1, d), lambda i: (0, 0))],
        out_specs=pl.BlockSpec((rb, d), lambda i: (i, 0)),
        out_shape=jax.ShapeDtypeStruct((m, d), BF16),
        compiler_params=_cparams(("arbitrary",), 32),
        name="norm_mid",
    )(x, nw)


def _norm_out_kernel(x_ref, nw_ref, yp_ref, ys_ref, *, n_prompt_blocks):
    i = pl.program_id(0)
    x = x_ref[...]
    y = x * lax.rsqrt(jnp.mean(x * x, axis=-1, keepdims=True) + EPS) * nw_ref[...]

    @pl.when(i < n_prompt_blocks)
    def _():
        yp_ref[...] = y

    @pl.when(i >= n_prompt_blocks)
    def _():
        ys_ref[...] = y


def _norm_out(x, nw, t, b):
    d = x.shape[1]
    npb, nsb = t // ROW_BLOCK, b // ROW_BLOCK
    return pl.pallas_call(
        functools.partial(_norm_out_kernel, n_prompt_blocks=npb),
        grid=(npb + nsb,),
        in_specs=[pl.BlockSpec((ROW_BLOCK, d), lambda i: (i, 0)),
                  pl.BlockSpec((1, d), lambda i: (0, 0))],
        out_specs=[
            pl.BlockSpec((ROW_BLOCK, d), lambda i: (jnp.minimum(i, npb - 1), 0)),
            pl.BlockSpec((ROW_BLOCK, d), lambda i: (jnp.maximum(i - npb, 0), 0)),
        ],
        out_shape=[jax.ShapeDtypeStruct((t, d), F32), jax.ShapeDtypeStruct((b, d), F32)],
        compiler_params=_cparams(("arbitrary",), 32),
        name="norm_out",
    )(x, nw)


def _row_tile(m, cap):
    best = 16
    for c in range(16, cap + 1, 16):
        if m % c == 0:
            best = c
    return best


def _proj_kernel(h_ref, w_ref, cos_ref, sin_ref, o_ref, *, n_attn_tiles, gate_tile0):
    j = pl.program_id(1)
    is_rope = jnp.logical_and(j < n_attn_tiles, j % 3 != 2)
    is_gate = j >= gate_tile0

    @pl.when(is_rope)
    def _():
        acc = _dot_nt(h_ref[...], w_ref[...])
        cos = cos_ref[...]
        sin = sin_ref[...]
        lane = lax.broadcasted_iota(jnp.int32, cos.shape, 1)
        for hh in range(A_HEADS):
            a = acc[:, hh * HEAD_DIM:(hh + 1) * HEAD_DIM]
            partner = jnp.where(lane < ROT_HALF,
                                pltpu.roll(a, HEAD_DIM - ROT_HALF, 1),
                                pltpu.roll(a, ROT_HALF, 1))
            o_ref[:, hh * HEAD_DIM:(hh + 1) * HEAD_DIM] = a * cos + partner * sin

    @pl.when(is_gate)
    def _():
        o_ref[...] = jax.nn.sigmoid(_dot_nt(h_ref[...], w_ref[...]))

    @pl.when(jnp.logical_not(jnp.logical_or(is_rope, is_gate)))
    def _():
        o_ref[...] = _dot_nt(h_ref[...], w_ref[...])


def _proj(h, w, cos, sin, gate_col0, gate_row0):
    m, k = h.shape
    skip = gate_row0 - gate_col0
    n = w.shape[0] - skip
    tn = A_WIDTH
    tm = _row_tile(m, 1040)
    gate_tile0 = gate_col0 // tn
    return pl.pallas_call(
        functools.partial(_proj_kernel, n_attn_tiles=A_COLS // tn, gate_tile0=gate_tile0),
        grid=(m // tm, n // tn),
        in_specs=[
            pl.BlockSpec((tm, k), lambda i, j: (i, 0)),
            pl.BlockSpec((pl.Element(tn), pl.Element(k)),
                         lambda i, j: (pl.multiple_of(j * tn + jnp.where(j >= gate_tile0, skip, 0),
                                                      2 * SUBLANES), 0)),
            pl.BlockSpec((tm, LANES), lambda i, j: (i, 0)),
            pl.BlockSpec((tm, LANES), lambda i, j: (i, 0)),
        ],
        out_specs=pl.BlockSpec((tm, tn), lambda i, j: (i, j)),
        out_shape=jax.ShapeDtypeStruct((m, n), F32),
        compiler_params=_cparams(("arbitrary", "arbitrary"), 48),
        name="proj_in",
    )(h, w, cos, sin)


def _merge_kernel(oa_ref, ob_ref, wa_ref, wb_ref, ga_ref, gb_ref, o_ref):
    a = _dot(oa_ref[...], wa_ref[...])
    b = _dot(ob_ref[...], wb_ref[...])
    o_ref[...] = _bf(ga_ref[...] * a + gb_ref[...] * b)


def _merge(oa, ob, wa, wb, proj, gate_col0):
    m = oa.shape[0]
    n = wa.shape[1]
    tn = COL_BLOCK
    tm = _row_tile(m, 1040)
    ga0 = gate_col0 // tn
    gb0 = (gate_col0 + n) // tn
    return pl.pallas_call(
        _merge_kernel,
        grid=(m // tm, n // tn),
        in_specs=[
            pl.BlockSpec((tm, oa.shape[1]), lambda i, j: (i, 0)),
            pl.BlockSpec((tm, ob.shape[1]), lambda i, j: (i, 0)),
            pl.BlockSpec((wa.shape[0], tn), lambda i, j: (0, j)),
            pl.BlockSpec((wb.shape[0], tn), lambda i, j: (0, j)),
            pl.BlockSpec((tm, tn), lambda i, j: (i, ga0 + j)),
            pl.BlockSpec((tm, tn), lambda i, j: (i, gb0 + j)),
        ],
        out_specs=pl.BlockSpec((tm, tn), lambda i, j: (i, j)),
        out_shape=jax.ShapeDtypeStruct((m, n), BF16),
        compiler_params=_cparams(("arbitrary", "arbitrary"), 40),
        name="merge",
    )(oa, ob, wa, wb, proj, proj)


def _mm_res_kernel(a_ref, w_ref, r_ref, o_ref):
    o_ref[...] = r_ref[...] + _dot(a_ref[...], w_ref[...])


def _mm_res(a, w, res, tm_cap, tn, name):
    m, k = a.shape
    n = w.shape[1]
    tm = _row_tile(m, tm_cap)
    return pl.pallas_call(
        _mm_res_kernel,
        grid=(m // tm, n // tn),
        in_specs=[
            pl.BlockSpec((tm, k), lambda i, j: (i, 0)),
            pl.BlockSpec((k, tn), lambda i, j: (0, j)),
            pl.BlockSpec((tm, tn), lambda i, j: (i, j)),
        ],
        out_specs=pl.BlockSpec((tm, tn), lambda i, j: (i, j)),
        out_shape=jax.ShapeDtypeStruct((m, n), F32),
        compiler_params=_cparams(("arbitrary", "arbitrary"), 56),
        name=name,
    )(a, w, res)


def _attn_prompt_kernel(*refs):
    grp = [refs[5 * g:5 * g + 5] for g in range(N_GROUPS)]
    o_ref, o_scr, l_scr = refs[5 * N_GROUPS:]
    first = pl.program_id(1) == 0
    row = lax.broadcasted_iota(jnp.int32, (LANES, LANES), 0)
    col = lax.broadcasted_iota(jnp.int32, (LANES, LANES), 1)
    mask_cur = col <= row
    mask_prev = col >= row
    mask_prev_first = col >= row + jnp.where(first, LANES, 0)
    scale = HEAD_DIM ** -0.5
    nq = KEYS_PER_QUERY

    for g, (_, dil) in enumerate(DIL_GROUPS):
        q_ref, k_ref, v_ref, kp_ref, vp_ref = grp[g]
        span = nq * dil

        def rows(ref, base, r, dil=dil):
            if dil == 1:
                return ref[pl.ds(base, nq), :]
            return ref[pl.ds(base + r, nq, stride=dil), :]

        for sb in range(SUPER // span):
            for r in range(dil):
                base = sb * span
                qt = _bf(rows(q_ref, base, r))
                kc = _bf(rows(k_ref, base, r))
                vc = _bf(rows(v_ref, base, r))
                if sb > 0:
                    kp = _bf(rows(k_ref, base - span, r))
                    vp = _bf(rows(v_ref, base - span, r))
                    mp = mask_prev
                else:
                    kp = _bf(rows(kp_ref, 0, r))
                    vp = _bf(rows(vp_ref, 0, r))
                    mp = mask_prev_first
                s_c = jnp.where(mask_cur, _dot_nt(qt, kc) * scale, -jnp.inf)
                s_p = jnp.where(mp, _dot_nt(qt, kp) * scale, -jnp.inf)
                mx = jnp.maximum(jnp.max(s_c, axis=-1, keepdims=True),
                                 jnp.max(s_p, axis=-1, keepdims=True))
                p_c = jnp.exp(s_c - mx)
                p_p = jnp.exp(s_p - mx)
                den = jnp.sum(p_c, axis=-1, keepdims=True) + jnp.sum(p_p, axis=-1, keepdims=True)
                o = (_dot(_bf(p_c), vc) + _dot(_bf(p_p), vp)) / den
                lse = jnp.broadcast_to(mx + jnp.log(den), (nq, LANES))
                if dil == 1:
                    o_scr[g, pl.ds(base, nq), :] = o
                    l_scr[g, pl.ds(base, nq), :] = lse
                else:
                    o_scr[g, pl.ds(base + r, nq, stride=dil), :] = o
                    l_scr[g, pl.ds(base + r, nq, stride=dil), :] = lse

    l0, l1, l2 = l_scr[0], l_scr[1], l_scr[2]
    mx = jnp.maximum(jnp.maximum(l0, l1), l2)
    e0, e1, e2 = jnp.exp(l0 - mx), jnp.exp(l1 - mx), jnp.exp(l2 - mx)
    o = (e0 * o_scr[0] + e1 * o_scr[1] + e2 * o_scr[2]) / (e0 + e1 + e2)
    o_ref[...] = _bf(o)


def _attn_prompt(proj, t):
    nsup = t // SUPER
    in_specs = []
    args = []
    for g, (_, dil) in enumerate(DIL_GROUPS):
        span = KEYS_PER_QUERY * dil
        per = SUPER // span
        cb = g * 3 * A_HEADS
        for part in range(3):
            in_specs.append(pl.BlockSpec(
                (SUPER, HEAD_DIM), lambda h, i, c=cb + part * A_HEADS: (i, c + h)))
            args.append(proj)
        for part in (1, 2):
            in_specs.append(pl.BlockSpec(
                (span, HEAD_DIM),
                lambda h, i, c=cb + part * A_HEADS, per=per: (jnp.maximum(i * per - 1, 0), c + h)))
            args.append(proj)
    return pl.pallas_call(
        _attn_prompt_kernel,
        grid=(A_HEADS, nsup),
        in_specs=in_specs,
        out_specs=pl.BlockSpec((SUPER, HEAD_DIM), lambda h, i: (i, h)),
        out_shape=jax.ShapeDtypeStruct((t, A_WIDTH), BF16),
        scratch_shapes=[pltpu.VMEM((N_GROUPS, SUPER, HEAD_DIM), F32),
                        pltpu.VMEM((N_GROUPS, SUPER, HEAD_DIM), F32)],
        compiler_params=_cparams(("arbitrary", "arbitrary"), 48),
        name="attn_prompt",
    )(*args)


def _attn_sample_kernel(qkv_ref, c0_ref, c1_ref, c2_ref, o_ref, *, bb):
    caches = (c0_ref, c1_ref, c2_ref)
    scale = HEAD_DIM ** -0.5
    sub = lax.broadcasted_iota(jnp.int32, (2 * A_HEADS, HEAD_DIM), 0)

    def on_sublanes(b, col0, first):
        out = jnp.zeros((2 * A_HEADS, HEAD_DIM), F32)
        for hh in range(A_HEADS):
            r = qkv_ref[b:b + 1, col0 + hh * HEAD_DIM:col0 + (hh + 1) * HEAD_DIM]
            out = jnp.where(sub == first + hh, r, out)
        return out

    for b in range(bb):
        outs, lses = [], []
        for g in range(N_GROUPS):
            c0 = g * 3 * A_WIDTH
            q8 = on_sublanes(b, c0, 0)
            kn8 = on_sublanes(b, c0 + A_WIDTH, 0)
            vn8 = on_sublanes(b, c0 + 2 * A_WIDTH, A_HEADS)
            kv = caches[g][b]
            s = jnp.sum(kv * q8, axis=-1, keepdims=True) * scale
            s = pltpu.roll(jnp.broadcast_to(s, kv.shape), A_HEADS, 1)
            sn = jnp.sum(kn8 * q8, axis=-1, keepdims=True) * scale
            sn = pltpu.roll(jnp.broadcast_to(sn, q8.shape), A_HEADS, 0)
            mx = jnp.maximum(jnp.max(s, axis=0), sn)
            p = jnp.exp(s - mx)
            pn = jnp.exp(sn - mx)
            den = jnp.sum(p, axis=0) + pn
            outs.append((jnp.sum(p * kv, axis=0) + pn * vn8) / den)
            lses.append(mx + jnp.log(den))
        mx = jnp.maximum(jnp.maximum(lses[0], lses[1]), lses[2])
        es = [jnp.exp(l - mx) for l in lses]
        o = (es[0] * outs[0] + es[1] * outs[1] + es[2] * outs[2]) / (es[0] + es[1] + es[2])
        for hh in range(A_HEADS):
            o_ref[b:b + 1, hh * HEAD_DIM:(hh + 1) * HEAD_DIM] = o[A_HEADS + hh:A_HEADS + hh + 1, :]


def _attn_sample(proj, caches, t, b):
    bb = SUBLANES
    in_specs = [pl.BlockSpec((bb, A_COLS), lambda i: (t // bb + i, 0))]
    args = [proj]
    for (win, dil), c in zip(DIL_GROUPS, caches):
        args.append(c.reshape(b, win // dil, dil, 2 * A_HEADS, HEAD_DIM))
        in_specs.append(pl.BlockSpec((bb, win // dil, None, 2 * A_HEADS, HEAD_DIM),
                                     lambda i: (i, 0, 0, 0, 0)))
    return pl.pallas_call(
        functools.partial(_attn_sample_kernel, bb=bb),
        grid=(b // bb,),
        in_specs=in_specs,
        out_specs=pl.BlockSpec((bb, A_WIDTH), lambda i: (i, 0)),
        out_shape=jax.ShapeDtypeStruct((b, A_WIDTH), F32),
        compiler_params=_cparams(("arbitrary",), 40),
        name="attn_sample",
    )(*args)


def _beta_decay_kernel(ba_ref, alog_ref, dtb_ref, ltri_ref, e_ref, beta_ref, gc_ref, *, cumulative):
    x = ba_ref[...]
    lane = lax.broadcasted_iota(jnp.int32, x.shape, 1)
    z = x + dtb_ref[...]
    softplus = jnp.maximum(z, 0.0) + jnp.log1p(jnp.exp(-jnp.abs(z)))
    g = -jnp.exp(alog_ref[...]) * softplus
    if cumulative:
        g = jnp.dot(ltri_ref[...], g, precision=HIGHEST, preferred_element_type=F32)
    vals = jnp.where(lane < B_HEADS, jax.nn.sigmoid(x), g)
    wide = jnp.dot(vals, e_ref[...], precision=HIGHEST, preferred_element_type=F32)
    beta_ref[...] = wide[:, :B_WIDTH]
    gc_ref[...] = wide[:, B_WIDTH:]


def _beta_decay(ba, alog_row, dtb_row, ltri, expand, row0, rows, cumulative):
    blk0 = row0 // CHUNK
    return pl.pallas_call(
        functools.partial(_beta_decay_kernel, cumulative=cumulative),
        grid=(rows // CHUNK,),
        in_specs=[
            pl.BlockSpec((CHUNK, LANES), lambda i: (blk0 + i, 0)),
            pl.BlockSpec((1, LANES), lambda i: (0, 0)),
            pl.BlockSpec((1, LANES), lambda i: (0, 0)),
            pl.BlockSpec((CHUNK, CHUNK), lambda i: (0, 0)),
            pl.BlockSpec((LANES, 2 * B_WIDTH), lambda i: (0, 0)),
        ],
        out_specs=[pl.BlockSpec((CHUNK, B_WIDTH), lambda i: (i, 0)),
                   pl.BlockSpec((CHUNK, B_WIDTH), lambda i: (i, 0))],
        out_shape=[jax.ShapeDtypeStruct((rows, B_WIDTH), F32),
                   jax.ShapeDtypeStruct((rows, B_WIDTH), F32)],
        compiler_params=_cparams(("arbitrary",), 32),
        name="beta_decay",
    )(ba, alog_row, dtb_row, ltri, expand)


def _l2norm_heads(x):
    parts = []
    for hh in range(x.shape[1] // HEAD_DIM):
        a = x[:, hh * HEAD_DIM:(hh + 1) * HEAD_DIM]
        parts.append(a * lax.rsqrt(jnp.sum(a * a, axis=-1, keepdims=True) + EPS))
    return jnp.concatenate(parts, axis=1)


def _conv_prompt_kernel(x_ref, cw_ref, beta_ref, o_ref, ob_ref, buf, *, rb):
    c = pl.program_id(0)
    i = pl.program_id(1)
    halo = SUBLANES

    @pl.when(i == 0)
    def _():
        buf[0:halo, :] = jnp.zeros((halo, buf.shape[1]), F32)

    buf[halo:halo + rb, :] = x_ref[...]
    acc = buf[pl.ds(halo, rb), :] * cw_ref[B_CONV - 1:B_CONV, :]
    for s in range(1, B_CONV):
        acc = acc + buf[pl.ds(halo - s, rb), :] * cw_ref[B_CONV - 1 - s:B_CONV - s, :]
    buf[0:halo, :] = buf[rb:rb + halo, :]
    y = _silu(acc)
    n_qk = 2 * B_HEADS // HEAD_GROUP
    beta = beta_ref[...]

    @pl.when(c < n_qk // 2)
    def _():
        o_ref[...] = _l2norm_heads(y) * (B_DK ** -0.5)
        ob_ref[...] = jnp.zeros_like(y)

    @pl.when(jnp.logical_and(c >= n_qk // 2, c < n_qk))
    def _():
        kn = _l2norm_heads(y)
        o_ref[...] = kn
        ob_ref[...] = kn * beta

    @pl.when(c >= n_qk)
    def _():
        o_ref[...] = y
        ob_ref[...] = y * beta


def _conv_prompt(proj, cw, beta, t, col0):
    rb = 256
    ncb = B_CONV_COLS // COL_BLOCK
    cb0 = col0 // COL_BLOCK
    hgs = B_HEADS // HEAD_GROUP
    return pl.pallas_call(
        functools.partial(_conv_prompt_kernel, rb=rb),
        grid=(ncb, t // rb),
        in_specs=[
            pl.BlockSpec((rb, COL_BLOCK), lambda c, i: (i, cb0 + c)),
            pl.BlockSpec((B_CONV, COL_BLOCK), lambda c, i: (0, c)),
            pl.BlockSpec((rb, COL_BLOCK), lambda c, i: (i, c % hgs)),
        ],
        out_specs=[pl.BlockSpec((rb, COL_BLOCK), lambda c, i: (i, c)),
                   pl.BlockSpec((rb, COL_BLOCK), lambda c, i: (i, c))],
        out_shape=[jax.ShapeDtypeStruct((t, B_CONV_COLS), F32),
                   jax.ShapeDtypeStruct((t, B_CONV_COLS), F32)],
        scratch_shapes=[pltpu.VMEM((rb + SUBLANES, COL_BLOCK), F32)],
        compiler_params=_cparams(("arbitrary", "arbitrary"), 32),
        name="conv_prompt",
    )(proj, cw, beta)


def _gated_norm(o, z, nw):
    y = o * lax.rsqrt(jnp.mean(o * o, axis=-1, keepdims=True) + EPS) * nw
    return y * _silu(z)


INV_BASE = 16


DELTA_HEADS = 16


def _unit_lower_inverse(a, eye, row, col):
    c = a[0].shape[0]

    def blk(size):
        sh = size.bit_length() - 1
        return (row >> sh) == (col >> sh)

    base = blk(INV_BASE)
    d = [jnp.where(base, x, 0.0) for x in a]
    tm = [eye - x for x in d]
    pw = [_bf(x) for x in d]
    for _ in range(INV_BASE.bit_length() - 2):
        pw = [_bf(_dot(x, x)) for x in pw]
        tm = [x + _dot(_bf(x), p) for x, p in zip(tm, pw)]
    size = INV_BASE
    while size < c:
        sel = jnp.logical_and(blk(2 * size), jnp.logical_not(blk(size)))
        tm_bf = [_bf(x) for x in tm]
        inner = [_bf(_dot(_bf(jnp.where(sel, x, 0.0)), y)) for x, y in zip(a, tm_bf)]
        tm = [x - _dot(y, z) for x, y, z in zip(tm, tm_bf, inner)]
        size *= 2
    return tm


def _delta_prompt_kernel(q_ref, k_ref, kb_ref, vb_ref, gc_ref, *rest):
    z_refs = rest[:DELTA_HEADS // HEAD_GROUP]
    nw_ref, o_ref, sfin_ref, s_scr = rest[DELTA_HEADS // HEAD_GROUP:]
    j = pl.program_id(1)
    c = CHUNK
    heads = range(DELTA_HEADS)
    sls = [slice(hh * LANES, (hh + 1) * LANES) for hh in heads]

    @pl.when(j == 0)
    def _():
        s_scr[...] = jnp.zeros_like(s_scr)

    row = lax.broadcasted_iota(jnp.int32, (c, c), 0)
    col = lax.broadcasted_iota(jnp.int32, (c, c), 1)
    tril = col <= row
    strict = col < row
    eye = (col == row).astype(F32)

    gc = [gc_ref[:, sl] for sl in sls]
    k = [k_ref[:, sl] for sl in sls]
    k_bf = [_bf(x) for x in k]
    kq = [_dot_nt(jnp.concatenate([_bf(kb_ref[:, sl]), _bf(q_ref[:, sl])], axis=0), kk)
          for sl, kk in zip(sls, k_bf)]
    decay = [jnp.exp(jnp.where(tril, g - g.T, -jnp.inf)) for g in gc]
    a = [jnp.where(strict, x[:c] * dd, 0.0) for x, dd in zip(kq, decay)]
    qk_bf = [_bf(x[c:] * dd) for x, dd in zip(kq, decay)]
    tm = _unit_lower_inverse(a, eye, row, col)
    e_g = [jnp.exp(g) for g in gc]
    uw = [_dot(_bf(t), jnp.concatenate([_bf(vb_ref[:, sl]), _bf(kb_ref[:, sl] * eg)], axis=1))
          for t, sl, eg in zip(tm, sls, e_g)]
    s = [s_scr[hh] for hh in heads]
    s_bf = [_bf(x) for x in s]
    ws = [_dot(jnp.concatenate([_bf(x[:, LANES:]), _bf(q_ref[:, sl] * eg)], axis=0), sb)
          for x, sl, eg, sb in zip(uw, sls, e_g, s_bf)]
    v_bf = [_bf(x[:, :LANES] - y[:c]) for x, y in zip(uw, ws)]
    o = [y[c:] + _dot(qq, vv) for y, qq, vv in zip(ws, qk_bf, v_bf)]
    g_last = [g[c - 1:c, :] for g in gc]
    kd_bf = [_bf(kk * jnp.exp(gl - g)) for kk, gl, g in zip(k, g_last, gc)]
    for hh in heads:
        s_scr[hh] = s[hh] * jnp.exp(g_last[hh]) + _dot_tn(kd_bf[hh], v_bf[hh])
        z = z_refs[hh // HEAD_GROUP][:, sls[hh % HEAD_GROUP]]
        o_ref[:, sls[hh]] = _gated_norm(o[hh], z, nw_ref[...]).astype(BF16)

    @pl.when(j == pl.num_programs(1) - 1)
    def _():
        sfin_ref[...] = s_scr[...]


def _delta_prompt(qkv, qkvb, gc, proj, nw, t, z_col0):
    width = DELTA_HEADS * LANES
    hgs = B_HEADS // DELTA_HEADS
    nz = DELTA_HEADS // HEAD_GROUP
    zb0 = z_col0 // COL_BLOCK
    blk = lambda off: pl.BlockSpec((CHUNK, width), lambda h, j, off=off: (j, off + h))
    zblk = lambda zi: pl.BlockSpec((CHUNK, COL_BLOCK), lambda h, j, zi=zi: (j, zb0 + h * nz + zi))
    return pl.pallas_call(
        _delta_prompt_kernel,
        grid=(hgs, t // CHUNK),
        in_specs=[blk(0), blk(hgs), blk(hgs), blk(2 * hgs), blk(0)] + [zblk(zi) for zi in range(nz)]
                 + [pl.BlockSpec((1, LANES), lambda h, j: (0, 0))],
        out_specs=[pl.BlockSpec((CHUNK, width), lambda h, j: (j, h)),
                   pl.BlockSpec((DELTA_HEADS, B_DK, B_DV), lambda h, j: (h, 0, 0))],
        out_shape=[jax.ShapeDtypeStruct((t, B_WIDTH), BF16),
                   jax.ShapeDtypeStruct((B_HEADS, B_DK, B_DV), F32)],
        scratch_shapes=[pltpu.VMEM((DELTA_HEADS, B_DK, B_DV), F32)],
        compiler_params=_cparams(("arbitrary", "arbitrary"), 32),
        name="delta_prompt",
    )(qkv, qkv, qkvb, qkvb, gc, *([proj] * nz), nw)


def _conv_sample_kernel(x_ref, s0_ref, s1_ref, s2_ref, cw_ref, o_ref):
    c = pl.program_id(0)
    acc = (s0_ref[...] * cw_ref[0:1, :] + s1_ref[...] * cw_ref[1:2, :]
           + s2_ref[...] * cw_ref[2:3, :] + x_ref[...] * cw_ref[3:4, :])
    y = _silu(acc)
    n_qk = 2 * B_HEADS // HEAD_GROUP

    @pl.when(c < n_qk // 2)
    def _():
        o_ref[...] = _l2norm_heads(y) * (B_DK ** -0.5)

    @pl.when(jnp.logical_and(c >= n_qk // 2, c < n_qk))
    def _():
        o_ref[...] = _l2norm_heads(y)

    @pl.when(c >= n_qk)
    def _():
        o_ref[...] = y


def _conv_sample(proj, state, cw, t, b, col0):
    ncb = B_CONV_COLS // COL_BLOCK
    cb0 = col0 // COL_BLOCK
    rb0 = t // b
    sspec = pl.BlockSpec((b, COL_BLOCK), lambda c: (0, c))
    return pl.pallas_call(
        _conv_sample_kernel,
        grid=(ncb,),
        in_specs=[pl.BlockSpec((b, COL_BLOCK), lambda c: (rb0, cb0 + c)), sspec, sspec, sspec,
                  pl.BlockSpec((B_CONV, COL_BLOCK), lambda c: (0, c))],
        out_specs=pl.BlockSpec((b, COL_BLOCK), lambda c: (0, c)),
        out_shape=jax.ShapeDtypeStruct((b, B_CONV_COLS), F32),
        compiler_params=_cparams(("arbitrary",), 32),
        name="conv_sample",
    )(proj, state[:, 0], state[:, 1], state[:, 2], cw)


def _delta_sample_kernel(q_ref, k_ref, v_ref, beta_ref, g_ref, z_ref, nw_ref, s_ref, o_ref, snew_ref, *, bb):
    row = lax.broadcasted_iota(jnp.int32, (LANES, LANES), 0)
    col = lax.broadcasted_iota(jnp.int32, (LANES, LANES), 1)
    eye = col == row

    def column(r):
        return jnp.sum(jnp.where(eye, r, 0.0), axis=-1, keepdims=True)

    for b in range(bb):
        for hh in range(HEAD_GROUP):
            sl = slice(hh * LANES, (hh + 1) * LANES)
            q, k, v = q_ref[b:b + 1, sl], k_ref[b:b + 1, sl], v_ref[b:b + 1, sl]
            beta = beta_ref[b:b + 1, sl]
            e_g = jnp.exp(g_ref[b:b + 1, sl])
            s = s_ref[b, hh]
            k_col = column(k)
            ks = jnp.sum(s * k_col, axis=0, keepdims=True)
            qs = jnp.sum(s * column(q), axis=0, keepdims=True)
            v_new = beta * (v - e_g * ks)
            qk = jnp.sum(q * k, axis=-1, keepdims=True)
            o = e_g * qs + qk * v_new
            snew_ref[b, hh] = s * e_g + k_col * v_new
            o_ref[b:b + 1, sl] = _gated_norm(o, z_ref[b:b + 1, sl], nw_ref[...])


def _delta_sample(qkv, beta, g, proj, nw, state, t, b, z_col0):
    bb = SUBLANES
    hgs = B_HEADS // HEAD_GROUP
    zb0 = z_col0 // COL_BLOCK
    rb0 = t // bb
    blk = lambda off: pl.BlockSpec((bb, COL_BLOCK), lambda i, h, off=off: (i, off + h))
    sspec = pl.BlockSpec((bb, HEAD_GROUP, B_DK, B_DV), lambda i, h: (i, h, 0, 0))
    return pl.pallas_call(
        functools.partial(_delta_sample_kernel, bb=bb),
        grid=(b // bb, hgs),
        in_specs=[blk(0), blk(hgs), blk(2 * hgs), blk(0), blk(0),
                  pl.BlockSpec((bb, COL_BLOCK), lambda i, h: (rb0 + i, zb0 + h)),
                  pl.BlockSpec((1, LANES), lambda i, h: (0, 0)), sspec],
        out_specs=[pl.BlockSpec((bb, COL_BLOCK), lambda i, h: (i, h)), sspec],
        out_shape=[jax.ShapeDtypeStruct((b, B_WIDTH), F32),
                   jax.ShapeDtypeStruct(state.shape, F32)],
        compiler_params=_cparams(("arbitrary", "arbitrary"), 32),
        name="delta_sample",
    )(qkv, qkv, qkv, beta, g, proj, nw, state)


def _ffn_prompt_kernel(h_ref, wg_ref, wu_ref, cw_ref, act_ref, tail_ref, buf, *, tm):
    i = pl.program_id(1)
    halo = SUBLANES

    @pl.when(i == 0)
    def _():
        buf[0:halo, :] = jnp.zeros((halo, buf.shape[1]), F32)

    h = h_ref[...]
    buf[halo:halo + tm, :] = _dot(h, wg_ref[...])
    acc = buf[pl.ds(halo, tm), :] * cw_ref[FFN_CONV - 1:FFN_CONV, :]
    for s in range(1, FFN_CONV):
        acc = acc + buf[pl.ds(halo - s, tm), :] * cw_ref[FFN_CONV - 1 - s:FFN_CONV - s, :]
    tail = buf[tm:tm + halo, :]
    buf[0:halo, :] = tail
    tail_ref[...] = tail
    act_ref[...] = _bf(_silu(acc) * _dot(h, wu_ref[...]))


def _ffn_prompt(h2, wg, wu, cw, t):
    m, d = h2.shape
    f = wg.shape[1]
    tf = 256
    tm = _row_tile(t, 1024)
    return pl.pallas_call(
        functools.partial(_ffn_prompt_kernel, tm=tm),
        grid=(f // tf, t // tm),
        in_specs=[
            pl.BlockSpec((tm, d), lambda c, i: (i, 0)),
            pl.BlockSpec((d, tf), lambda c, i: (0, c)),
            pl.BlockSpec((d, tf), lambda c, i: (0, c)),
            pl.BlockSpec((FFN_CONV, tf), lambda c, i: (0, c)),
        ],
        out_specs=[pl.BlockSpec((tm, tf), lambda c, i: (i, c)),
                   pl.BlockSpec((SUBLANES, tf), lambda c, i: (0, c))],
        out_shape=[jax.ShapeDtypeStruct((m, f), BF16),
                   jax.ShapeDtypeStruct((SUBLANES, f), F32)],
        scratch_shapes=[pltpu.VMEM((tm + SUBLANES, tf), F32)],
        compiler_params=_cparams(("arbitrary", "arbitrary"), 48),
        name="ffn_prompt",
    )(h2, wg, wu, cw)


def _ffn_sample_kernel(h_ref, wg_ref, wu_ref, cw_ref, s0_ref, s1_ref, act_in_ref, act_ref, g_ref):
    del act_in_ref
    h = h_ref[...]
    g = _dot(h, wg_ref[...])
    g_ref[...] = g
    acc = s0_ref[...] * cw_ref[0:1, :] + s1_ref[...] * cw_ref[1:2, :] + g * cw_ref[2:3, :]
    act_ref[...] = _bf(_silu(acc) * _dot(h, wu_ref[...]))


def _ffn_sample(h2, wg, wu, cw, state, act, t, b):
    d = h2.shape[1]
    f = wg.shape[1]
    tf = 256
    rb0 = t // b
    sspec = pl.BlockSpec((b, tf), lambda c: (0, c))
    return pl.pallas_call(
        _ffn_sample_kernel,
        grid=(f // tf,),
        in_specs=[
            pl.BlockSpec((b, d), lambda c: (rb0, 0)),
            pl.BlockSpec((d, tf), lambda c: (0, c)),
            pl.BlockSpec((d, tf), lambda c: (0, c)),
            pl.BlockSpec((FFN_CONV, tf), lambda c: (0, c)),
            sspec, sspec,
            pl.BlockSpec(memory_space=pl.ANY),
        ],
        out_specs=[pl.BlockSpec((b, tf), lambda c: (rb0, c)),
                   pl.BlockSpec((b, tf), lambda c: (0, c))],
        out_shape=[jax.ShapeDtypeStruct(act.shape, BF16),
                   jax.ShapeDtypeStruct((b, f), F32)],
        input_output_aliases={6: 0},
        compiler_params=_cparams(("arbitrary",), 32),
        name="ffn_sample",
    )(h2, wg, wu, cw, state[:, 0], state[:, 1], act)


def _rope_tables(t, b):
    inv = ROPE_THETA ** (-jnp.arange(0, ROT_DIM, 2, dtype=F32) / ROT_DIM)
    pos = jnp.concatenate([jnp.arange(t, dtype=jnp.int32),
                           jnp.full((b,), PAST_LEN, jnp.int32)]).astype(F32)
    ang = pos[:, None] * inv[None, :]
    cos, sin = jnp.cos(ang), jnp.sin(ang)
    m = t + b
    pad = LANES - ROT_DIM
    cos_t = jnp.concatenate([cos, cos, jnp.ones((m, pad), F32)], axis=1)
    sin_t = jnp.concatenate([-sin, sin, jnp.zeros((m, pad), F32)], axis=1)
    return cos_t, sin_t


def kernel(x_prompt, x_sample, cache_kv_w128, cache_kv_w512, cache_kv_w2048, state_conv_qkv, state_delta,
           state_ffn_conv, norm_mix, w_in, conv_qkv, a_log, dt_bias, delta_norm, w_a_out, w_b_out, w_out,
           norm_ffn, w_gate, ffn_conv, w_up, w_down, norm_final):
    assert norm_mix.shape[0] == 1, "single layer only"
    assert x_prompt.shape[0] == 1 and x_sample.shape[1] == 1
    t, d = x_prompt.shape[1], x_prompt.shape[2]
    b = x_sample.shape[0]
    assert t % SUPER == 0 and b == ROW_BLOCK
    caches = (cache_kv_w128[0], cache_kv_w512[0], cache_kv_w2048[0])
    for (win, _), c in zip(DIL_GROUPS, caches):
        assert c.shape[1] == win, "cache must hold a full window"

    off_bqkv = A_COLS
    off_bz = off_bqkv + B_CONV_COLS
    off_ba = off_bz + B_WIDTH
    off_gate = off_ba + 2 * B_HEADS
    assert w_in.shape[2] == off_gate + 2 * d
    w_in_t = jnp.swapaxes(w_in[0], 0, 1)
    w_main = _bf(w_in_t)
    w_ba = jnp.pad(w_in_t[off_ba:off_gate], ((0, LANES - 2 * B_HEADS), (0, 0)))
    gate_col0 = off_ba

    xp = x_prompt.reshape(t, d)
    xs = x_sample.reshape(b, d)
    h, x_cat, ba = _norm_in(xp, xs, norm_mix, w_ba)
    cos_t, sin_t = _rope_tables(t, b)
    proj = _proj(h, w_main, cos_t, sin_t, gate_col0, off_gate)

    oa_p = _attn_prompt(proj, t)
    oa_s = _attn_sample(proj, caches, t, b)
    o_a = jnp.concatenate([oa_p, _bf(oa_s)], axis=0)

    pad_heads = (0, LANES - 2 * B_HEADS)
    alog_row = jnp.pad(a_log[0], (B_HEADS, LANES - 2 * B_HEADS)).reshape(1, LANES)
    dtb_row = jnp.pad(dt_bias[0], (B_HEADS, LANES - 2 * B_HEADS)).reshape(1, LANES)
    del pad_heads
    ltri = jnp.tril(jnp.ones((CHUNK, CHUNK), F32))
    head_of_col = jnp.arange(2 * B_WIDTH, dtype=jnp.int32) // LANES
    expand = (jnp.arange(LANES, dtype=jnp.int32)[:, None] == head_of_col[None, :]).astype(F32)
    beta_p, gc_p = _beta_decay(ba, alog_row, dtb_row, ltri, expand, 0, t, True)
    beta_s, g_s = _beta_decay(ba, alog_row, dtb_row, ltri, expand, t, b, False)
    cw_qkv = conv_qkv[0]
    qkv_p, qkvb_p = _conv_prompt(proj, cw_qkv, beta_p, t, off_bqkv)
    ob_p, delta_p = _delta_prompt(qkv_p, qkvb_p, gc_p, proj, delta_norm, t, off_bz)
    qkv_s = _conv_sample(proj, state_conv_qkv[0], cw_qkv, t, b, off_bqkv)
    ob_s, delta_s = _delta_sample(qkv_s, beta_s, g_s, proj, delta_norm, state_delta[0], t, b, off_bz)
    o_b = jnp.concatenate([ob_p, _bf(ob_s)], axis=0)

    merged = _merge(o_a, o_b, _bf(w_a_out[0]), _bf(w_b_out[0]), proj, gate_col0)
    x1 = _mm_res(merged, _bf(w_out[0]), x_cat, 1040, 512, "out_proj")

    h2 = _norm_mid(x1, norm_ffn)
    wg, wu = _bf(w_gate[0]), _bf(w_up[0])
    act, g_tail = _ffn_prompt(h2, wg, wu, ffn_conv[0], t)
    act, g_smp = _ffn_sample(h2, wg, wu, ffn_conv[0], state_ffn_conv[0], act, t, b)
    x2 = _mm_res(act, _bf(w_down[0]), x1, 640, 256, "down_proj")
    y_p, y_s = _norm_out(x2, norm_final.reshape(1, d), t, b)

    kv_p, kv_s = [], []
    for g, (win, _) in enumerate(DIL_GROUPS):
        c0 = g * 3 * A_WIDTH + A_WIDTH
        keep = min(win, t)
        kv_p.append(proj[t - keep:t, c0:c0 + 2 * A_WIDTH].reshape(1, 1, keep, 2, A_HEADS, HEAD_DIM))
        kv_s.append(proj[t:, c0:c0 + 2 * A_WIDTH].reshape(1, b, 1, 2, A_HEADS, HEAD_DIM))
    conv_p = proj[t - (B_CONV - 1):t, off_bqkv:off_bz].reshape(1, 1, B_CONV - 1, B_CONV_COLS)
    conv_s = jnp.concatenate([state_conv_qkv[0][:, 1:], proj[t:, off_bqkv:off_bz][:, None]], axis=1)[None]
    ffn_p = g_tail[SUBLANES - (FFN_CONV - 1):].reshape(1, 1, FFN_CONV - 1, -1)
    ffn_s = jnp.concatenate([state_ffn_conv[0][:, 1:], g_smp[:, None]], axis=1)[None]
    return (y_p.reshape(1, t, d), y_s.reshape(b, 1, d), kv_p[0], kv_p[1], kv_p[2],
            conv_p, delta_p.reshape(1, 1, B_HEADS, B_DK, B_DV), ffn_p,
            kv_s[0], kv_s[1], kv_s[2], conv_s, delta_s[None], ffn_s)
```

```python
import functools

import jax
import jax.numpy as jnp
from jax import lax
from jax.experimental import pallas as pl
from jax.experimental.pallas import tpu as pltpu

F32 = jnp.float32
BF16 = jnp.bfloat16
HIGHEST = lax.Precision.HIGHEST

HEAD_DIM = 128
ROT_DIM = HEAD_DIM // 4
ROT_HALF = ROT_DIM // 2
ROPE_THETA = 500000.0
DIL_GROUPS = ((128, 1), (512, 4), (2048, 16))
N_GROUPS = len(DIL_GROUPS)
A_HEADS = 4
A_WIDTH = A_HEADS * HEAD_DIM
A_COLS = N_GROUPS * 3 * A_WIDTH
KEYS_PER_QUERY = 128
B_HEADS = 16
B_DK = 128
B_DV = 128
B_WIDTH = B_HEADS * B_DK
B_CONV = 4
B_CONV_COLS = 3 * B_WIDTH
FFN_CONV = 3
EPS = 1e-6
PAST_LEN = 2048

LANES = 128
SUBLANES = 8
ROW_BLOCK = 128
SUPER = 2048
CHUNK = 128
HEAD_GROUP = 4
COL_BLOCK = HEAD_GROUP * LANES
VMEM_CAP_MB = 60


def _cparams(sem, vmem_mb):
    return pltpu.CompilerParams(dimension_semantics=sem,
                                vmem_limit_bytes=int(min(vmem_mb, VMEM_CAP_MB) * 2 ** 20))


def _bf(x):
    return x.astype(BF16)


def _dot(a, b):
    return jnp.dot(a, b, preferred_element_type=F32)


def _dot_nt(a, b):
    return lax.dot_general(a, b, (((1,), (1,)), ((), ())), preferred_element_type=F32)


def _dot_tn(a, b):
    return lax.dot_general(a, b, (((0,), (0,)), ((), ())), preferred_element_type=F32)


def _silu(x):
    return x * jax.nn.sigmoid(x)


def _norm_in_kernel(xp_ref, xs_ref, nw_ref, wba_ref, h_ref, xcat_ref, ba_ref, *, n_prompt_blocks):
    i = pl.program_id(0)

    def body(x):
        y = x * lax.rsqrt(jnp.mean(x * x, axis=-1, keepdims=True) + EPS) * nw_ref[...]
        h_ref[...] = _bf(y)
        xcat_ref[...] = x
        ba_ref[...] = lax.dot_general(y, wba_ref[...], (((1,), (1,)), ((), ())),
                                      precision=HIGHEST, preferred_element_type=F32)

    @pl.when(i < n_prompt_blocks)
    def _():
        body(xp_ref[...])

    @pl.when(i >= n_prompt_blocks)
    def _():
        body(xs_ref[...])


def _norm_in(xp, xs, nw, wba):
    t, d = xp.shape
    b = xs.shape[0]
    npb, nsb = t // ROW_BLOCK, b // ROW_BLOCK
    m = t + b
    return pl.pallas_call(
        functools.partial(_norm_in_kernel, n_prompt_blocks=npb),
        grid=(npb + nsb,),
        in_specs=[
            pl.BlockSpec((ROW_BLOCK, d), lambda i: (jnp.minimum(i, npb - 1), 0)),
            pl.BlockSpec((ROW_BLOCK, d), lambda i: (jnp.maximum(i - npb, 0), 0)),
            pl.BlockSpec((1, d), lambda i: (0, 0)),
            pl.BlockSpec((LANES, d), lambda i: (0, 0)),
        ],
        out_specs=[
            pl.BlockSpec((ROW_BLOCK, d), lambda i: (i, 0)),
            pl.BlockSpec((ROW_BLOCK, d), lambda i: (i, 0)),
            pl.BlockSpec((ROW_BLOCK, LANES), lambda i: (i, 0)),
        ],
        out_shape=[
            jax.ShapeDtypeStruct((m, d), BF16),
            jax.ShapeDtypeStruct((m, d), F32),
            jax.ShapeDtypeStruct((m, LANES), F32),
        ],
        compiler_params=_cparams(("arbitrary",), 32),
        name="norm_in",
    )(xp, xs, nw, wba)


def _norm_mid_kernel(x_ref, nw_ref, h_ref):
    x = x_ref[...]
    y = x * lax.rsqrt(jnp.mean(x * x, axis=-1, keepdims=True) + EPS) * nw_ref[...]
    h_ref[...] = _bf(y)


def _norm_mid(x, nw):
    m, d = x.shape
    rb = 2 * ROW_BLOCK if m % (2 * ROW_BLOCK) == 0 else ROW_BLOCK
    return pl.pallas_call(
        _norm_mid_kernel,
        grid=(m // rb,),
        in_specs=[pl.BlockSpec((rb, d), lambda i: (i, 0)),
                  pl.BlockSpec((1, d), lambda i: (0, 0))],
        out_specs=pl.BlockSpec((rb, d), lambda i: (i, 0)),
        out_shape=jax.ShapeDtypeStruct((m, d), BF16),
        compiler_params=_cparams(("arbitrary",), 32),
        name="norm_mid",
    )(x, nw)


def _norm_out_kernel(x_ref, nw_ref, yp_ref, ys_ref, *, n_prompt_blocks):
    i = pl.program_id(0)
    x = x_ref[...]
    y = x * lax.rsqrt(jnp.mean(x * x, axis=-1, keepdims=True) + EPS) * nw_ref[...]

    @pl.when(i < n_prompt_blocks)
    def _():
        yp_ref[...] = y

    @pl.when(i >= n_prompt_blocks)
    def _():
        ys_ref[...] = y


def _norm_out(x, nw, t, b):
    d = x.shape[1]
    npb, nsb = t // ROW_BLOCK, b // ROW_BLOCK
    return pl.pallas_call(
        functools.partial(_norm_out_kernel, n_prompt_blocks=npb),
        grid=(npb + nsb,),
        in_specs=[pl.BlockSpec((ROW_BLOCK, d), lambda i: (i, 0)),
                  pl.BlockSpec((1, d), lambda i: (0, 0))],
        out_specs=[
            pl.BlockSpec((ROW_BLOCK, d), lambda i: (jnp.minimum(i, npb - 1), 0)),
            pl.BlockSpec((ROW_BLOCK, d), lambda i: (jnp.maximum(i - npb, 0), 0)),
        ],
        out_shape=[jax.ShapeDtypeStruct((t, d), F32), jax.ShapeDtypeStruct((b, d), F32)],
        compiler_params=_cparams(("arbitrary",), 32),
        name="norm_out",
    )(x, nw)


def _row_tile(m, cap):
    best = 16
    for c in range(16, cap + 1, 16):
        if m % c == 0:
            best = c
    return best


def _proj_kernel(h_ref, w_ref, cos_ref, sin_ref, o_ref, *, n_attn_tiles, gate_tile0):
    j = pl.program_id(1)
    is_rope = jnp.logical_and(j < n_attn_tiles, j % 3 != 2)
    is_gate = j >= gate_tile0

    @pl.when(is_rope)
    def _():
        acc = _dot_nt(h_ref[...], w_ref[...])
        cos = cos_ref[...]
        sin = sin_ref[...]
        lane = lax.broadcasted_iota(jnp.int32, cos.shape, 1)
        for hh in range(A_HEADS):
            a = acc[:, hh * HEAD_DIM:(hh + 1) * HEAD_DIM]
            partner = jnp.where(lane < ROT_HALF,
                                pltpu.roll(a, HEAD_DIM - ROT_HALF, 1),
                                pltpu.roll(a, ROT_HALF, 1))
            o_ref[:, hh * HEAD_DIM:(hh + 1) * HEAD_DIM] = a * cos + partner * sin

    @pl.when(is_gate)
    def _():
        o_ref[...] = jax.nn.sigmoid(_dot_nt(h_ref[...], w_ref[...]))

    @pl.when(jnp.logical_not(jnp.logical_or(is_rope, is_gate)))
    def _():
        o_ref[...] = _dot_nt(h_ref[...], w_ref[...])


def _proj(h, w, cos, sin, gate_col0, gate_row0):
    m, k = h.shape
    skip = gate_row0 - gate_col0
    n = w.shape[0] - skip
    tn = A_WIDTH
    tm = _row_tile(m, 1040)
    gate_tile0 = gate_col0 // tn
    return pl.pallas_call(
        functools.partial(_proj_kernel, n_attn_tiles=A_COLS // tn, gate_tile0=gate_tile0),
        grid=(m // tm, n // tn),
        in_specs=[
            pl.BlockSpec((tm, k), lambda i, j: (i, 0)),
            pl.BlockSpec((pl.Element(tn), pl.Element(k)),
                         lambda i, j: (pl.multiple_of(j * tn + jnp.where(j >= gate_tile0, skip, 0),
                                                      2 * SUBLANES), 0)),
            pl.BlockSpec((tm, LANES), lambda i, j: (i, 0)),
            pl.BlockSpec((tm, LANES), lambda i, j: (i, 0)),
        ],
        out_specs=pl.BlockSpec((tm, tn), lambda i, j: (i, j)),
        out_shape=jax.ShapeDtypeStruct((m, n), F32),
        compiler_params=_cparams(("arbitrary", "arbitrary"), 48),
        name="proj_in",
    )(h, w, cos, sin)


def _merge_kernel(oa_ref, ob_ref, wa_ref, wb_ref, ga_ref, gb_ref, o_ref):
    a = _dot(oa_ref[...], wa_ref[...])
    b = _dot(ob_ref[...], wb_ref[...])
    o_ref[...] = _bf(ga_ref[...] * a + gb_ref[...] * b)


def _merge(oa, ob, wa, wb, proj, gate_col0):
    m = oa.shape[0]
    n = wa.shape[1]
    tn = COL_BLOCK
    tm = _row_tile(m, 1040)
    ga0 = gate_col0 // tn
    gb0 = (gate_col0 + n) // tn
    return pl.pallas_call(
        _merge_kernel,
        grid=(m // tm, n // tn),
        in_specs=[
            pl.BlockSpec((tm, oa.shape[1]), lambda i, j: (i, 0)),
            pl.BlockSpec((tm, ob.shape[1]), lambda i, j: (i, 0)),
            pl.BlockSpec((wa.shape[0], tn), lambda i, j: (0, j)),
            pl.BlockSpec((wb.shape[0], tn), lambda i, j: (0, j)),
            pl.BlockSpec((tm, tn), lambda i, j: (i, ga0 + j)),
            pl.BlockSpec((tm, tn), lambda i, j: (i, gb0 + j)),
        ],
        out_specs=pl.BlockSpec((tm, tn), lambda i, j: (i, j)),
        out_shape=jax.ShapeDtypeStruct((m, n), BF16),
        compiler_params=_cparams(("arbitrary", "arbitrary"), 40),
        name="merge",
    )(oa, ob, wa, wb, proj, proj)


def _mm_res_kernel(a_ref, w_ref, r_ref, o_ref):
    o_ref[...] = r_ref[...] + _dot(a_ref[...], w_ref[...])


def _mm_res(a, w, res, tm_cap, tn, name):
    m, k = a.shape
    n = w.shape[1]
    tm = _row_tile(m, tm_cap)
    return pl.pallas_call(
        _mm_res_kernel,
        grid=(m // tm, n // tn),
        in_specs=[
            pl.BlockSpec((tm, k), lambda i, j: (i, 0)),
            pl.BlockSpec((k, tn), lambda i, j: (0, j)),
            pl.BlockSpec((tm, tn), lambda i, j: (i, j)),
        ],
        out_specs=pl.BlockSpec((tm, tn), lambda i, j: (i, j)),
        out_shape=jax.ShapeDtypeStruct((m, n), F32),
        compiler_params=_cparams(("arbitrary", "arbitrary"), 56),
        name=name,
    )(a, w, res)


ATTN_TILE_BATCH = 8


def _attn_prompt_kernel(*refs):
    grp = [refs[5 * g:5 * g + 5] for g in range(N_GROUPS)]
    o_ref, o_scr, l_scr = refs[5 * N_GROUPS:]
    first = pl.program_id(1) == 0
    row = lax.broadcasted_iota(jnp.int32, (LANES, LANES), 0)
    col = lax.broadcasted_iota(jnp.int32, (LANES, LANES), 1)
    mask_cur = col <= row
    mask_prev = col >= row
    mask_prev_first = col >= row + jnp.where(first, LANES, 0)
    scale = HEAD_DIM ** -0.5
    nq = KEYS_PER_QUERY

    def rows(dil, base, r):
        if dil == 1:
            return pl.ds(base, nq)
        return pl.ds(base + r, nq, stride=dil)

    tiles = [(g, dil, sb, r) for g, (_, dil) in enumerate(DIL_GROUPS)
             for sb in range(SUPER // (nq * dil)) for r in range(dil)]
    for t0 in range(0, len(tiles), ATTN_TILE_BATCH):
        batch = tiles[t0:t0 + ATTN_TILE_BATCH]
        scores = []
        for g, dil, sb, r in batch:
            q_ref, k_ref, v_ref, kp_ref, vp_ref = grp[g]
            span = nq * dil
            cur = rows(dil, sb * span, r)
            qt = _bf(q_ref[cur, :])
            if sb > 0:
                prev = rows(dil, (sb - 1) * span, r)
                kp, vp, mp = k_ref[prev, :], v_ref[prev, :], mask_prev
            else:
                prev = rows(dil, 0, r)
                kp, vp, mp = kp_ref[prev, :], vp_ref[prev, :], mask_prev_first
            s_c = jnp.where(mask_cur, _dot_nt(qt, _bf(k_ref[cur, :])) * scale, -jnp.inf)
            s_p = jnp.where(mp, _dot_nt(qt, _bf(kp)) * scale, -jnp.inf)
            scores.append((s_c, s_p, _bf(v_ref[cur, :]), _bf(vp)))
        probs = []
        for s_c, s_p, vc, vp in scores:
            mx = jnp.maximum(jnp.max(s_c, axis=-1, keepdims=True),
                             jnp.max(s_p, axis=-1, keepdims=True))
            p_c = jnp.exp(s_c - mx)
            p_p = jnp.exp(s_p - mx)
            den = jnp.sum(p_c, axis=-1, keepdims=True) + jnp.sum(p_p, axis=-1, keepdims=True)
            probs.append((_bf(p_c), _bf(p_p), vc, vp, den, mx))
        for (g, dil, sb, r), (p_c, p_p, vc, vp, den, mx) in zip(batch, probs):
            o = (_dot(p_c, vc) + _dot(p_p, vp)) / den
            dst = rows(dil, sb * nq * dil, r)
            o_scr[g, dst, :] = o
            l_scr[g, dst, :] = jnp.broadcast_to(mx + jnp.log(den), (nq, LANES))

    l0, l1, l2 = l_scr[0], l_scr[1], l_scr[2]
    mx = jnp.maximum(jnp.maximum(l0, l1), l2)
    e0, e1, e2 = jnp.exp(l0 - mx), jnp.exp(l1 - mx), jnp.exp(l2 - mx)
    o = (e0 * o_scr[0] + e1 * o_scr[1] + e2 * o_scr[2]) / (e0 + e1 + e2)
    o_ref[...] = _bf(o)


def _attn_prompt(proj, t):
    nsup = t // SUPER
    in_specs = []
    args = []
    for g, (_, dil) in enumerate(DIL_GROUPS):
        span = KEYS_PER_QUERY * dil
        per = SUPER // span
        cb = g * 3 * A_HEADS
        for part in range(3):
            in_specs.append(pl.BlockSpec(
                (SUPER, HEAD_DIM), lambda h, i, c=cb + part * A_HEADS: (i, c + h)))
            args.append(proj)
        for part in (1, 2):
            in_specs.append(pl.BlockSpec(
                (span, HEAD_DIM),
                lambda h, i, c=cb + part * A_HEADS, per=per: (jnp.maximum(i * per - 1, 0), c + h)))
            args.append(proj)
    return pl.pallas_call(
        _attn_prompt_kernel,
        grid=(A_HEADS, nsup),
        in_specs=in_specs,
        out_specs=pl.BlockSpec((SUPER, HEAD_DIM), lambda h, i: (i, h)),
        out_shape=jax.ShapeDtypeStruct((t, A_WIDTH), BF16),
        scratch_shapes=[pltpu.VMEM((N_GROUPS, SUPER, HEAD_DIM), F32),
                        pltpu.VMEM((N_GROUPS, SUPER, HEAD_DIM), F32)],
        compiler_params=_cparams(("arbitrary", "arbitrary"), 48),
        name="attn_prompt",
    )(*args)


def _attn_sample_kernel(qkv_ref, c0_ref, c1_ref, c2_ref, o_ref, *, bb):
    caches = (c0_ref, c1_ref, c2_ref)
    scale = HEAD_DIM ** -0.5
    sub = lax.broadcasted_iota(jnp.int32, (2 * A_HEADS, HEAD_DIM), 0)

    def on_sublanes(b, col0, first):
        out = jnp.zeros((2 * A_HEADS, HEAD_DIM), F32)
        for hh in range(A_HEADS):
            r = qkv_ref[b:b + 1, col0 + hh * HEAD_DIM:col0 + (hh + 1) * HEAD_DIM]
            out = jnp.where(sub == first + hh, r, out)
        return out

    for b in range(bb):
        outs, lses = [], []
        for g in range(N_GROUPS):
            c0 = g * 3 * A_WIDTH
            q8 = on_sublanes(b, c0, 0)
            kn8 = on_sublanes(b, c0 + A_WIDTH, 0)
            vn8 = on_sublanes(b, c0 + 2 * A_WIDTH, A_HEADS)
            kv = caches[g][b]
            s = jnp.sum(kv * q8, axis=-1, keepdims=True) * scale
            s = pltpu.roll(jnp.broadcast_to(s, kv.shape), A_HEADS, 1)
            sn = jnp.sum(kn8 * q8, axis=-1, keepdims=True) * scale
            sn = pltpu.roll(jnp.broadcast_to(sn, q8.shape), A_HEADS, 0)
            mx = jnp.maximum(jnp.max(s, axis=0), sn)
            p = jnp.exp(s - mx)
            pn = jnp.exp(sn - mx)
            den = jnp.sum(p, axis=0) + pn
            outs.append((jnp.sum(p * kv, axis=0) + pn * vn8) / den)
            lses.append(mx + jnp.log(den))
        mx = jnp.maximum(jnp.maximum(lses[0], lses[1]), lses[2])
        es = [jnp.exp(l - mx) for l in lses]
        o = (es[0] * outs[0] + es[1] * outs[1] + es[2] * outs[2]) / (es[0] + es[1] + es[2])
        for hh in range(A_HEADS):
            o_ref[b:b + 1, hh * HEAD_DIM:(hh + 1) * HEAD_DIM] = o[A_HEADS + hh:A_HEADS + hh + 1, :]


def _attn_sample(proj, caches, t, b):
    bb = SUBLANES
    in_specs = [pl.BlockSpec((bb, A_COLS), lambda i: (t // bb + i, 0))]
    args = [proj]
    for (win, dil), c in zip(DIL_GROUPS, caches):
        args.append(c.reshape(b, win // dil, dil, 2 * A_HEADS, HEAD_DIM))
        in_specs.append(pl.BlockSpec((bb, win // dil, None, 2 * A_HEADS, HEAD_DIM),
                                     lambda i: (i, 0, 0, 0, 0)))
    return pl.pallas_call(
        functools.partial(_attn_sample_kernel, bb=bb),
        grid=(b // bb,),
        in_specs=in_specs,
        out_specs=pl.BlockSpec((bb, A_WIDTH), lambda i: (i, 0)),
        out_shape=jax.ShapeDtypeStruct((b, A_WIDTH), F32),
        compiler_params=_cparams(("arbitrary",), 40),
        name="attn_sample",
    )(*args)


def _beta_decay_kernel(ba_ref, alog_ref, dtb_ref, ltri_ref, e_ref, beta_ref, gc_ref, *, cumulative):
    x = ba_ref[...]
    lane = lax.broadcasted_iota(jnp.int32, x.shape, 1)
    z = x + dtb_ref[...]
    softplus = jnp.maximum(z, 0.0) + jnp.log1p(jnp.exp(-jnp.abs(z)))
    g = -jnp.exp(alog_ref[...]) * softplus
    if cumulative:
        g = jnp.dot(ltri_ref[...], g, precision=HIGHEST, preferred_element_type=F32)
    vals = jnp.where(lane < B_HEADS, jax.nn.sigmoid(x), g)
    wide = jnp.dot(vals, e_ref[...], precision=HIGHEST, preferred_element_type=F32)
    beta_ref[...] = wide[:, :B_WIDTH]
    gc_ref[...] = wide[:, B_WIDTH:]


def _beta_decay(ba, alog_row, dtb_row, ltri, expand, row0, rows, cumulative):
    blk0 = row0 // CHUNK
    return pl.pallas_call(
        functools.partial(_beta_decay_kernel, cumulative=cumulative),
        grid=(rows // CHUNK,),
        in_specs=[
            pl.BlockSpec((CHUNK, LANES), lambda i: (blk0 + i, 0)),
            pl.BlockSpec((1, LANES), lambda i: (0, 0)),
            pl.BlockSpec((1, LANES), lambda i: (0, 0)),
            pl.BlockSpec((CHUNK, CHUNK), lambda i: (0, 0)),
            pl.BlockSpec((LANES, 2 * B_WIDTH), lambda i: (0, 0)),
        ],
        out_specs=[pl.BlockSpec((CHUNK, B_WIDTH), lambda i: (i, 0)),
                   pl.BlockSpec((CHUNK, B_WIDTH), lambda i: (i, 0))],
        out_shape=[jax.ShapeDtypeStruct((rows, B_WIDTH), F32),
                   jax.ShapeDtypeStruct((rows, B_WIDTH), F32)],
        compiler_params=_cparams(("arbitrary",), 32),
        name="beta_decay",
    )(ba, alog_row, dtb_row, ltri, expand)


def _l2norm_heads(x):
    parts = []
    for hh in range(x.shape[1] // HEAD_DIM):
        a = x[:, hh * HEAD_DIM:(hh + 1) * HEAD_DIM]
        parts.append(a * lax.rsqrt(jnp.sum(a * a, axis=-1, keepdims=True) + EPS))
    return jnp.concatenate(parts, axis=1)


def _gated_norm(o, z, nw):
    y = o * lax.rsqrt(jnp.mean(o * o, axis=-1, keepdims=True) + EPS) * nw
    return y * _silu(z)


INV_BASE = 16


assert CHUNK == LANES


def _unit_lower_inverse(a, eye, row, col):
    c = a[0].shape[0]

    def blk(size):
        sh = size.bit_length() - 1
        return (row >> sh) == (col >> sh)

    base = blk(INV_BASE)
    d = [jnp.where(base, x, 0.0) for x in a]
    tm = [eye - x for x in d]
    pw = [_bf(x) for x in d]
    for _ in range(INV_BASE.bit_length() - 2):
        pw = [_bf(_dot(x, x)) for x in pw]
        tm = [x + _dot(_bf(x), p) for x, p in zip(tm, pw)]
    size = INV_BASE
    while size < c:
        sel = jnp.logical_and(blk(2 * size), jnp.logical_not(blk(size)))
        tm_bf = [_bf(x) for x in tm]
        inner = [_bf(_dot(_bf(jnp.where(sel, x, 0.0)), y)) for x, y in zip(a, tm_bf)]
        tm = [x - _dot(y, z) for x, y, z in zip(tm, tm_bf, inner)]
        size *= 2
    return tm


def _delta_prompt_kernel(x_ref, z_ref, ba_ref, alog_ref, dtb_ref, ltri_ref, cw_ref, nw_ref,
                         o_ref, sfin_ref, s_scr, buf):
    j = pl.program_id(0)
    c = CHUNK
    halo = SUBLANES
    heads = range(B_HEADS)

    @pl.when(j == 0)
    def _():
        s_scr[...] = jnp.zeros_like(s_scr)
        buf[0:halo, :] = jnp.zeros((halo, buf.shape[1]), F32)

    row = lax.broadcasted_iota(jnp.int32, (c, c), 0)
    col = lax.broadcasted_iota(jnp.int32, (c, c), 1)
    tril = col <= row
    strict = col < row
    eye = (col == row).astype(F32)

    logits = ba_ref[...]
    zz = logits + dtb_ref[...]
    softplus = jnp.maximum(zz, 0.0) + jnp.log1p(jnp.exp(-jnp.abs(zz)))
    g_cum = jnp.dot(ltri_ref[...], -jnp.exp(alog_ref[...]) * softplus,
                    precision=HIGHEST, preferred_element_type=F32)
    lane = lax.broadcasted_iota(jnp.int32, logits.shape, 1)
    per_head = jnp.where(lane < B_HEADS, jax.nn.sigmoid(logits), g_cum).T
    gc_t = [jnp.broadcast_to(per_head[B_HEADS + hh:B_HEADS + hh + 1, :], (c, c)) for hh in heads]
    gc = [x.T for x in gc_t]
    beta = [jnp.broadcast_to(per_head[hh:hh + 1, :], (c, c)).T for hh in heads]

    buf[halo:halo + c, :] = x_ref[...]

    def conv(c0):
        cs = slice(c0, c0 + LANES)
        acc = buf[pl.ds(halo, c), cs] * cw_ref[B_CONV - 1:B_CONV, cs]
        for s in range(1, B_CONV):
            acc = acc + buf[pl.ds(halo - s, c), cs] * cw_ref[B_CONV - 1 - s:B_CONV - s, cs]
        return _silu(acc)

    def l2n(a):
        return a * lax.rsqrt(jnp.sum(a * a, axis=-1, keepdims=True) + EPS)

    q = [l2n(conv(hh * LANES)) * (B_DK ** -0.5) for hh in heads]
    k = [l2n(conv(B_WIDTH + hh * LANES)) for hh in heads]
    v = [conv(2 * B_WIDTH + hh * LANES) for hh in heads]
    buf[0:halo, :] = buf[c:c + halo, :]

    k_bf = [_bf(x) for x in k]
    kb = [x * bb for x, bb in zip(k, beta)]
    kq = [_dot_nt(jnp.concatenate([_bf(x), _bf(y)], axis=0), kk) for x, y, kk in zip(kb, q, k_bf)]
    decay = [jnp.exp(jnp.where(tril, g - gt, -jnp.inf)) for g, gt in zip(gc, gc_t)]
    a = [jnp.where(strict, x[:c] * dd, 0.0) for x, dd in zip(kq, decay)]
    qk_bf = [_bf(x[c:] * dd) for x, dd in zip(kq, decay)]
    tm = _unit_lower_inverse(a, eye, row, col)
    e_g = [jnp.exp(g) for g in gc]
    uw = [_dot(_bf(t), jnp.concatenate([_bf(vv * bb), _bf(x * eg)], axis=1))
          for t, vv, bb, x, eg in zip(tm, v, beta, kb, e_g)]
    s = [s_scr[hh] for hh in heads]
    s_bf = [_bf(x) for x in s]
    ws = [_dot(jnp.concatenate([_bf(x[:, LANES:]), _bf(qq * eg)], axis=0), sb)
          for x, qq, eg, sb in zip(uw, q, e_g, s_bf)]
    v_bf = [_bf(x[:, :LANES] - y[:c]) for x, y in zip(uw, ws)]
    o = [y[c:] + _dot(qq, vv) for y, qq, vv in zip(ws, qk_bf, v_bf)]
    g_last = [g[c - 1:c, :] for g in gc]
    kd_bf = [_bf(kk * jnp.exp(gl - g)) for kk, gl, g in zip(k, g_last, gc)]
    for hh in heads:
        sl = slice(hh * LANES, (hh + 1) * LANES)
        s_scr[hh] = s[hh] * jnp.exp(g_last[hh]) + _dot_tn(kd_bf[hh], v_bf[hh])
        o_ref[:, sl] = _gated_norm(o[hh], z_ref[:, sl], nw_ref[...]).astype(BF16)

    @pl.when(j == pl.num_programs(0) - 1)
    def _():
        sfin_ref[...] = s_scr[...]


def _delta_prompt(proj, ba, alog_row, dtb_row, ltri, cw, nw, t, x_col0, z_col0):
    c = CHUNK
    const = lambda shape: pl.BlockSpec(shape, lambda j: (0,) * len(shape))
    return pl.pallas_call(
        _delta_prompt_kernel,
        grid=(t // c,),
        in_specs=[
            pl.BlockSpec((pl.Element(c), pl.Element(B_CONV_COLS)),
                         lambda j: (pl.multiple_of(j * c, c), x_col0)),
            pl.BlockSpec((pl.Element(c), pl.Element(B_WIDTH)),
                         lambda j: (pl.multiple_of(j * c, c), z_col0)),
            pl.BlockSpec((c, LANES), lambda j: (j, 0)),
            const((1, LANES)), const((1, LANES)), const((c, c)), const((B_CONV, B_CONV_COLS)),
            const((1, LANES)),
        ],
        out_specs=[pl.BlockSpec((c, B_WIDTH), lambda j: (j, 0)),
                   const((B_HEADS, B_DK, B_DV))],
        out_shape=[jax.ShapeDtypeStruct((t, B_WIDTH), BF16),
                   jax.ShapeDtypeStruct((B_HEADS, B_DK, B_DV), F32)],
        scratch_shapes=[pltpu.VMEM((B_HEADS, B_DK, B_DV), F32),
                        pltpu.VMEM((c + SUBLANES, B_CONV_COLS), F32)],
        compiler_params=_cparams(("arbitrary",), 40),
        name="delta_prompt",
    )(proj, proj, ba, alog_row, dtb_row, ltri, cw, nw)


def _conv_sample_kernel(x_ref, s0_ref, s1_ref, s2_ref, cw_ref, o_ref):
    c = pl.program_id(0)
    acc = (s0_ref[...] * cw_ref[0:1, :] + s1_ref[...] * cw_ref[1:2, :]
           + s2_ref[...] * cw_ref[2:3, :] + x_ref[...] * cw_ref[3:4, :])
    y = _silu(acc)
    n_qk = 2 * B_HEADS // HEAD_GROUP

    @pl.when(c < n_qk // 2)
    def _():
        o_ref[...] = _l2norm_heads(y) * (B_DK ** -0.5)

    @pl.when(jnp.logical_and(c >= n_qk // 2, c < n_qk))
    def _():
        o_ref[...] = _l2norm_heads(y)

    @pl.when(c >= n_qk)
    def _():
        o_ref[...] = y


def _conv_sample(proj, state, cw, t, b, col0):
    ncb = B_CONV_COLS // COL_BLOCK
    cb0 = col0 // COL_BLOCK
    rb0 = t // b
    sspec = pl.BlockSpec((b, COL_BLOCK), lambda c: (0, c))
    return pl.pallas_call(
        _conv_sample_kernel,
        grid=(ncb,),
        in_specs=[pl.BlockSpec((b, COL_BLOCK), lambda c: (rb0, cb0 + c)), sspec, sspec, sspec,
                  pl.BlockSpec((B_CONV, COL_BLOCK), lambda c: (0, c))],
        out_specs=pl.BlockSpec((b, COL_BLOCK), lambda c: (0, c)),
        out_shape=jax.ShapeDtypeStruct((b, B_CONV_COLS), F32),
        compiler_params=_cparams(("arbitrary",), 32),
        name="conv_sample",
    )(proj, state[:, 0], state[:, 1], state[:, 2], cw)


def _delta_sample_kernel(q_ref, k_ref, v_ref, beta_ref, g_ref, z_ref, nw_ref, s_ref, o_ref, snew_ref, *, bb):
    row = lax.broadcasted_iota(jnp.int32, (LANES, LANES), 0)
    col = lax.broadcasted_iota(jnp.int32, (LANES, LANES), 1)
    eye = col == row

    def column(r):
        return jnp.sum(jnp.where(eye, r, 0.0), axis=-1, keepdims=True)

    units = [(b, hh, slice(hh * LANES, (hh + 1) * LANES)) for b in range(bb) for hh in range(HEAD_GROUP)]
    rows = [(q_ref[b:b + 1, sl], k_ref[b:b + 1, sl]) for b, _, sl in units]
    cols = [(column(q), column(k)) for q, k in rows]
    qk = [jnp.sum(q * k, axis=-1, keepdims=True) for q, k in rows]
    reads = [(jnp.sum(s_ref[b, hh] * qc, axis=0, keepdims=True),
              jnp.sum(s_ref[b, hh] * kc, axis=0, keepdims=True))
             for (b, hh, _), (qc, kc) in zip(units, cols)]
    outs = []
    for (b, hh, sl), (_, k_col), (qs, ks), qk_u in zip(units, cols, reads, qk):
        e_g = jnp.exp(g_ref[b:b + 1, sl])
        v_new = beta_ref[b:b + 1, sl] * (v_ref[b:b + 1, sl] - e_g * ks)
        outs.append(e_g * qs + qk_u * v_new)
        snew_ref[b, hh] = s_ref[b, hh] * e_g + k_col * v_new
    for (b, _, sl), o in zip(units, outs):
        o_ref[b:b + 1, sl] = _gated_norm(o, z_ref[b:b + 1, sl], nw_ref[...])


def _delta_sample(qkv, beta, g, proj, nw, state, t, b, z_col0):
    bb = SUBLANES
    hgs = B_HEADS // HEAD_GROUP
    zb0 = z_col0 // COL_BLOCK
    rb0 = t // bb
    blk = lambda off: pl.BlockSpec((bb, COL_BLOCK), lambda i, h, off=off: (i, off + h))
    sspec = pl.BlockSpec((bb, HEAD_GROUP, B_DK, B_DV), lambda i, h: (i, h, 0, 0))
    return pl.pallas_call(
        functools.partial(_delta_sample_kernel, bb=bb),
        grid=(b // bb, hgs),
        in_specs=[blk(0), blk(hgs), blk(2 * hgs), blk(0), blk(0),
                  pl.BlockSpec((bb, COL_BLOCK), lambda i, h: (rb0 + i, zb0 + h)),
                  pl.BlockSpec((1, LANES), lambda i, h: (0, 0)), sspec],
        out_specs=[pl.BlockSpec((bb, COL_BLOCK), lambda i, h: (i, h)), sspec],
        out_shape=[jax.ShapeDtypeStruct((b, B_WIDTH), F32),
                   jax.ShapeDtypeStruct(state.shape, F32)],
        compiler_params=_cparams(("arbitrary", "arbitrary"), 32),
        name="delta_sample",
    )(qkv, qkv, qkv, beta, g, proj, nw, state)


def _ffn_kernel(h_ref, wg_ref, wu_ref, cw_ref, s0_ref, s1_ref, act_ref, tail_ref, gs_ref,
                wg_bf, wu_bf, buf, *, tm, n_prompt_tiles, bs):
    i = pl.program_id(1)
    halo = SUBLANES

    @pl.when(i == 0)
    def _():
        wg_bf[...] = _bf(wg_ref[...])
        wu_bf[...] = _bf(wu_ref[...])
        buf[0:halo, :] = jnp.zeros((halo, buf.shape[1]), F32)

    @pl.when(i < n_prompt_tiles)
    def _():
        h = h_ref[...]
        buf[halo:halo + tm, :] = _dot(h, wg_bf[...])
        acc = buf[pl.ds(halo, tm), :] * cw_ref[FFN_CONV - 1:FFN_CONV, :]
        for s in range(1, FFN_CONV):
            acc = acc + buf[pl.ds(halo - s, tm), :] * cw_ref[FFN_CONV - 1 - s:FFN_CONV - s, :]
        tail = buf[tm:tm + halo, :]
        buf[0:halo, :] = tail
        tail_ref[...] = tail
        act_ref[...] = _bf(_silu(acc) * _dot(h, wu_bf[...]))

    @pl.when(i == n_prompt_tiles)
    def _():
        h = h_ref[0:bs, :]
        g = _dot(h, wg_bf[...])
        gs_ref[...] = g
        acc = s0_ref[...] * cw_ref[0:1, :] + s1_ref[...] * cw_ref[1:2, :] + g * cw_ref[2:3, :]
        act_ref[0:bs, :] = _bf(_silu(acc) * _dot(h, wu_bf[...]))


def _ffn(h2, wg, wu, cw, state, t, b):
    m, d = h2.shape
    f = wg.shape[1]
    tf = 256
    tm = _row_tile(t, 1024)
    npt = t // tm
    assert b <= tm and m == t + b
    sspec = pl.BlockSpec((b, tf), lambda c, i: (0, c))
    return pl.pallas_call(
        functools.partial(_ffn_kernel, tm=tm, n_prompt_tiles=npt, bs=b),
        grid=(f // tf, npt + 1),
        in_specs=[
            pl.BlockSpec((tm, d), lambda c, i: (i, 0)),
            pl.BlockSpec((d, tf), lambda c, i: (0, c)),
            pl.BlockSpec((d, tf), lambda c, i: (0, c)),
            pl.BlockSpec((FFN_CONV, tf), lambda c, i: (0, c)),
            sspec, sspec,
        ],
        out_specs=[pl.BlockSpec((tm, tf), lambda c, i: (i, c)),
                   pl.BlockSpec((SUBLANES, tf), lambda c, i: (0, c)),
                   pl.BlockSpec((b, tf), lambda c, i: (0, c))],
        out_shape=[jax.ShapeDtypeStruct((m, f), BF16),
                   jax.ShapeDtypeStruct((SUBLANES, f), F32),
                   jax.ShapeDtypeStruct((b, f), F32)],
        scratch_shapes=[pltpu.VMEM((d, tf), BF16), pltpu.VMEM((d, tf), BF16),
                        pltpu.VMEM((tm + SUBLANES, tf), F32)],
        compiler_params=_cparams(("arbitrary", "arbitrary"), 56),
        name="ffn",
    )(h2, wg, wu, cw, state[:, 0], state[:, 1])


def _rope_tables(t, b):
    inv = ROPE_THETA ** (-jnp.arange(0, ROT_DIM, 2, dtype=F32) / ROT_DIM)
    pos = jnp.concatenate([jnp.arange(t, dtype=jnp.int32),
                           jnp.full((b,), PAST_LEN, jnp.int32)]).astype(F32)
    ang = pos[:, None] * inv[None, :]
    cos, sin = jnp.cos(ang), jnp.sin(ang)
    m = t + b
    pad = LANES - ROT_DIM
    cos_t = jnp.concatenate([cos, cos, jnp.ones((m, pad), F32)], axis=1)
    sin_t = jnp.concatenate([-sin, sin, jnp.zeros((m, pad), F32)], axis=1)
    return cos_t, sin_t


def kernel(x_prompt, x_sample, cache_kv_w128, cache_kv_w512, cache_kv_w2048, state_conv_qkv, state_delta,
           state_ffn_conv, norm_mix, w_in, conv_qkv, a_log, dt_bias, delta_norm, w_a_out, w_b_out, w_out,
           norm_ffn, w_gate, ffn_conv, w_up, w_down, norm_final):
    assert norm_mix.shape[0] == 1, "single layer only"
    assert x_prompt.shape[0] == 1 and x_sample.shape[1] == 1
    t, d = x_prompt.shape[1], x_prompt.shape[2]
    b = x_sample.shape[0]
    assert t % SUPER == 0 and b == ROW_BLOCK
    caches = (cache_kv_w128[0], cache_kv_w512[0], cache_kv_w2048[0])
    for (win, _), c in zip(DIL_GROUPS, caches):
        assert c.shape[1] == win, "cache must hold a full window"

    off_bqkv = A_COLS
    off_bz = off_bqkv + B_CONV_COLS
    off_ba = off_bz + B_WIDTH
    off_gate = off_ba + 2 * B_HEADS
    assert w_in.shape[2] == off_gate + 2 * d
    w_in_t = jnp.swapaxes(w_in[0], 0, 1)
    w_main = _bf(w_in_t)
    w_ba = jnp.pad(w_in_t[off_ba:off_gate], ((0, LANES - 2 * B_HEADS), (0, 0)))
    gate_col0 = off_ba

    xp = x_prompt.reshape(t, d)
    xs = x_sample.reshape(b, d)
    h, x_cat, ba = _norm_in(xp, xs, norm_mix, w_ba)
    cos_t, sin_t = _rope_tables(t, b)
    proj = _proj(h, w_main, cos_t, sin_t, gate_col0, off_gate)

    oa_p = _attn_prompt(proj, t)
    oa_s = _attn_sample(proj, caches, t, b)
    o_a = jnp.concatenate([oa_p, _bf(oa_s)], axis=0)

    alog_row = jnp.pad(a_log[0], (B_HEADS, LANES - 2 * B_HEADS)).reshape(1, LANES)
    dtb_row = jnp.pad(dt_bias[0], (B_HEADS, LANES - 2 * B_HEADS)).reshape(1, LANES)
    ltri = jnp.tril(jnp.ones((CHUNK, CHUNK), F32))
    head_of_col = jnp.arange(2 * B_WIDTH, dtype=jnp.int32) // LANES
    expand = (jnp.arange(LANES, dtype=jnp.int32)[:, None] == head_of_col[None, :]).astype(F32)
    cw_qkv = conv_qkv[0]
    ob_p, delta_p = _delta_prompt(proj, ba, alog_row, dtb_row, ltri, cw_qkv, delta_norm, t, off_bqkv, off_bz)
    beta_s, g_s = _beta_decay(ba, alog_row, dtb_row, ltri, expand, t, b, False)
    qkv_s = _conv_sample(proj, state_conv_qkv[0], cw_qkv, t, b, off_bqkv)
    ob_s, delta_s = _delta_sample(qkv_s, beta_s, g_s, proj, delta_norm, state_delta[0], t, b, off_bz)
    o_b = jnp.concatenate([ob_p, _bf(ob_s)], axis=0)

    merged = _merge(o_a, o_b, _bf(w_a_out[0]), _bf(w_b_out[0]), proj, gate_col0)
    x1 = _mm_res(merged, _bf(w_out[0]), x_cat, 1040, 512, "out_proj")

    h2 = _norm_mid(x1, norm_ffn)
    act, g_tail, g_smp = _ffn(h2, w_gate[0], w_up[0], ffn_conv[0], state_ffn_conv[0], t, b)
    x2 = _mm_res(act, _bf(w_down[0]), x1, 640, 256, "down_proj")
    y_p, y_s = _norm_out(x2, norm_final.reshape(1, d), t, b)

    kv_p, kv_s = [], []
    for g, (win, _) in enumerate(DIL_GROUPS):
        c0 = g * 3 * A_WIDTH + A_WIDTH
        keep = min(win, t)
        kv_p.append(proj[t - keep:t, c0:c0 + 2 * A_WIDTH].reshape(1, 1, keep, 2, A_HEADS, HEAD_DIM))
        kv_s.append(proj[t:, c0:c0 + 2 * A_WIDTH].reshape(1, b, 1, 2, A_HEADS, HEAD_DIM))
    conv_p = proj[t - (B_CONV - 1):t, off_bqkv:off_bz].reshape(1, 1, B_CONV - 1, B_CONV_COLS)
    conv_s = jnp.concatenate([state_conv_qkv[0][:, 1:], proj[t:, off_bqkv:off_bz][:, None]], axis=1)[None]
    ffn_p = g_tail[SUBLANES - (FFN_CONV - 1):].reshape(1, 1, FFN_CONV - 1, -1)
    ffn_s = jnp.concatenate([state_ffn_conv[0][:, 1:], g_smp[:, None]], axis=1)[None]
    return (y_p.reshape(1, t, d), y_s.reshape(b, 1, d), kv_p[0], kv_p[1], kv_p[2],
            conv_p, delta_p.reshape(1, 1, B_HEADS, B_DK, B_DV), ffn_p,
            kv_s[0], kv_s[1], kv_s[2], conv_s, delta_s[None], ffn_s)
```

```python
import functools

import jax
import jax.numpy as jnp
from jax import lax
from jax.experimental import pallas as pl
from jax.experimental.pallas import tpu as pltpu

F32 = jnp.float32
BF16 = jnp.bfloat16
HIGHEST = lax.Precision.HIGHEST

HEAD_DIM = 128
ROT_DIM = HEAD_DIM // 4
ROT_HALF = ROT_DIM // 2
ROPE_THETA = 500000.0
DIL_GROUPS = ((128, 1), (512, 4), (2048, 16))
N_GROUPS = len(DIL_GROUPS)
A_HEADS = 4
A_WIDTH = A_HEADS * HEAD_DIM
A_COLS = N_GROUPS * 3 * A_WIDTH
KEYS_PER_QUERY = 128
B_HEADS = 16
B_DK = 128
B_DV = 128
B_WIDTH = B_HEADS * B_DK
B_CONV = 4
B_CONV_COLS = 3 * B_WIDTH
FFN_CONV = 3
EPS = 1e-6
PAST_LEN = 2048

LANES = 128
SUBLANES = 8
ROW_BLOCK = 128
SUPER = 2048
CHUNK = 128
HEAD_GROUP = 4
COL_BLOCK = HEAD_GROUP * LANES
VMEM_CAP_MB = 60
ROW_TILE_CAP = 1664


def _cparams(sem, vmem_mb):
    return pltpu.CompilerParams(dimension_semantics=sem,
                                vmem_limit_bytes=int(min(vmem_mb, VMEM_CAP_MB) * 2 ** 20))


def _bf(x):
    return x.astype(BF16)


def _dot(a, b):
    return jnp.dot(a, b, preferred_element_type=F32)


def _dot_nt(a, b):
    return lax.dot_general(a, b, (((1,), (1,)), ((), ())), preferred_element_type=F32)


def _dot_tn(a, b):
    return lax.dot_general(a, b, (((0,), (0,)), ((), ())), preferred_element_type=F32)


def _silu(x):
    return x * jax.nn.sigmoid(x)


NORM_ROWS = 256


def _norm_in_kernel(xp_ref, xs_ref, nw_ref, wba_ref, h_ref, xcat_ref, ba_ref, *, n_prompt_blocks, bs):
    i = pl.program_id(0)

    def body(x, rows):
        y = x * lax.rsqrt(jnp.mean(x * x, axis=-1, keepdims=True) + EPS) * nw_ref[...]
        y_hi = _bf(y)
        y_lo = _bf(y - y_hi.astype(F32))
        h_ref[rows, :] = y_hi
        xcat_ref[rows, :] = x
        hi = _dot_nt(y_hi, wba_ref[...])
        ba_ref[rows, :] = hi[:, :LANES] + hi[:, LANES:] + _dot_nt(y_lo, wba_ref[0:LANES, :])

    @pl.when(i < n_prompt_blocks)
    def _():
        body(xp_ref[...], slice(None))

    @pl.when(i == n_prompt_blocks)
    def _():
        body(xs_ref[...], slice(0, bs))


def _norm_in(xp, xs, nw, wba):
    t, d = xp.shape
    b = xs.shape[0]
    rb = NORM_ROWS
    npb = t // rb
    m = t + b
    return pl.pallas_call(
        functools.partial(_norm_in_kernel, n_prompt_blocks=npb, bs=b),
        grid=(npb + 1,),
        in_specs=[
            pl.BlockSpec((rb, d), lambda i: (jnp.minimum(i, npb - 1), 0)),
            pl.BlockSpec((b, d), lambda i: (0, 0)),
            pl.BlockSpec((1, d), lambda i: (0, 0)),
            pl.BlockSpec((2 * LANES, d), lambda i: (0, 0)),
        ],
        out_specs=[
            pl.BlockSpec((rb, d), lambda i: (i, 0)),
            pl.BlockSpec((rb, d), lambda i: (i, 0)),
            pl.BlockSpec((rb, LANES), lambda i: (i, 0)),
        ],
        out_shape=[
            jax.ShapeDtypeStruct((m, d), BF16),
            jax.ShapeDtypeStruct((m, d), F32),
            jax.ShapeDtypeStruct((m, LANES), F32),
        ],
        compiler_params=_cparams(("arbitrary",), 40),
        name="norm_in",
    )(xp, xs, nw, wba)


def _norm_mid_kernel(x_ref, nw_ref, h_ref):
    x = x_ref[...]
    y = x * lax.rsqrt(jnp.mean(x * x, axis=-1, keepdims=True) + EPS) * nw_ref[...]
    h_ref[...] = _bf(y)


def _norm_mid(x, nw):
    m, d = x.shape
    rb = _row_tile(m, 640)
    return pl.pallas_call(
        _norm_mid_kernel,
        grid=(m // rb,),
        in_specs=[pl.BlockSpec((rb, d), lambda i: (i, 0)),
                  pl.BlockSpec((1, d), lambda i: (0, 0))],
        out_specs=pl.BlockSpec((rb, d), lambda i: (i, 0)),
        out_shape=jax.ShapeDtypeStruct((m, d), BF16),
        compiler_params=_cparams(("arbitrary",), 40),
        name="norm_mid",
    )(x, nw)


def _norm_out_kernel(x_ref, nw_ref, yp_ref, ys_ref, *, n_prompt_blocks, bs):
    i = pl.program_id(0)

    def norm(x):
        return x * lax.rsqrt(jnp.mean(x * x, axis=-1, keepdims=True) + EPS) * nw_ref[...]

    @pl.when(i < n_prompt_blocks)
    def _():
        yp_ref[...] = norm(x_ref[...])

    @pl.when(i == n_prompt_blocks)
    def _():
        ys_ref[...] = norm(x_ref[0:bs, :])


def _norm_out(x, nw, t, b):
    d = x.shape[1]
    rb = NORM_ROWS
    npb = t // rb
    return pl.pallas_call(
        functools.partial(_norm_out_kernel, n_prompt_blocks=npb, bs=b),
        grid=(npb + 1,),
        in_specs=[pl.BlockSpec((rb, d), lambda i: (i, 0)),
                  pl.BlockSpec((1, d), lambda i: (0, 0))],
        out_specs=[
            pl.BlockSpec((rb, d), lambda i: (jnp.minimum(i, npb - 1), 0)),
            pl.BlockSpec((b, d), lambda i: (0, 0)),
        ],
        out_shape=[jax.ShapeDtypeStruct((t, d), F32), jax.ShapeDtypeStruct((b, d), F32)],
        compiler_params=_cparams(("arbitrary",), 32),
        name="norm_out",
    )(x, nw)


def _row_tile(m, cap):
    best = 16
    for c in range(16, cap + 1, 16):
        if m % c == 0:
            best = c
    return best


def _proj_kernel(h_ref, w_ref, cos_ref, sin_ref, o_ref, *, n_attn_tiles, gate_tile0):
    j = pl.program_id(1)
    is_rope = jnp.logical_and(j < n_attn_tiles, j % 3 != 2)
    is_gate = j >= gate_tile0

    @pl.when(is_rope)
    def _():
        acc = _dot_nt(h_ref[...], w_ref[...])
        cos = cos_ref[...]
        sin = sin_ref[...]
        lane = lax.broadcasted_iota(jnp.int32, cos.shape, 1)
        for hh in range(A_HEADS):
            a = acc[:, hh * HEAD_DIM:(hh + 1) * HEAD_DIM]
            partner = jnp.where(lane < ROT_HALF,
                                pltpu.roll(a, HEAD_DIM - ROT_HALF, 1),
                                pltpu.roll(a, ROT_HALF, 1))
            o_ref[:, hh * HEAD_DIM:(hh + 1) * HEAD_DIM] = a * cos + partner * sin

    @pl.when(is_gate)
    def _():
        o_ref[...] = jax.nn.sigmoid(_dot_nt(h_ref[...], w_ref[...]))

    @pl.when(jnp.logical_not(jnp.logical_or(is_rope, is_gate)))
    def _():
        o_ref[...] = _dot_nt(h_ref[...], w_ref[...])


def _proj(h, w, cos, sin, gate_col0, gate_row0):
    m, k = h.shape
    skip = gate_row0 - gate_col0
    n = w.shape[0] - skip
    tn = A_WIDTH
    tm = _row_tile(m, ROW_TILE_CAP)
    gate_tile0 = gate_col0 // tn
    return pl.pallas_call(
        functools.partial(_proj_kernel, n_attn_tiles=A_COLS // tn, gate_tile0=gate_tile0),
        grid=(m // tm, n // tn),
        in_specs=[
            pl.BlockSpec((tm, k), lambda i, j: (i, 0)),
            pl.BlockSpec((pl.Element(tn), pl.Element(k)),
                         lambda i, j: (pl.multiple_of(j * tn + jnp.where(j >= gate_tile0, skip, 0),
                                                      2 * SUBLANES), 0)),
            pl.BlockSpec((tm, LANES), lambda i, j: (i, 0)),
            pl.BlockSpec((tm, LANES), lambda i, j: (i, 0)),
        ],
        out_specs=pl.BlockSpec((tm, tn), lambda i, j: (i, j)),
        out_shape=jax.ShapeDtypeStruct((m, n), F32),
        compiler_params=_cparams(("arbitrary", "arbitrary"), 54),
        name="proj_in",
    )(h, w, cos, sin)


def _merge_kernel(oa_ref, ob_ref, wa_ref, wb_ref, ga_ref, gb_ref, o_ref):
    a = _dot(oa_ref[...], wa_ref[...])
    b = _dot(ob_ref[...], wb_ref[...])
    o_ref[...] = _bf(ga_ref[...] * a + gb_ref[...] * b)


def _merge(oa, ob, wa, wb, proj, gate_col0):
    m = oa.shape[0]
    n = wa.shape[1]
    tn = COL_BLOCK
    tm = _row_tile(m, ROW_TILE_CAP)
    ga0 = gate_col0 // tn
    gb0 = (gate_col0 + n) // tn
    return pl.pallas_call(
        _merge_kernel,
        grid=(m // tm, n // tn),
        in_specs=[
            pl.BlockSpec((tm, oa.shape[1]), lambda i, j: (i, 0)),
            pl.BlockSpec((tm, ob.shape[1]), lambda i, j: (i, 0)),
            pl.BlockSpec((wa.shape[0], tn), lambda i, j: (0, j)),
            pl.BlockSpec((wb.shape[0], tn), lambda i, j: (0, j)),
            pl.BlockSpec((tm, tn), lambda i, j: (i, ga0 + j)),
            pl.BlockSpec((tm, tn), lambda i, j: (i, gb0 + j)),
        ],
        out_specs=pl.BlockSpec((tm, tn), lambda i, j: (i, j)),
        out_shape=jax.ShapeDtypeStruct((m, n), BF16),
        compiler_params=_cparams(("arbitrary", "arbitrary"), 48),
        name="merge",
    )(oa, ob, wa, wb, proj, proj)


def _mm_res_kernel(a_ref, w_ref, r_ref, o_ref):
    o_ref[...] = r_ref[...] + _dot(a_ref[...], w_ref[...])


def _mm_res(a, w, res, tm_cap, tn, name):
    m, k = a.shape
    n = w.shape[1]
    tm = _row_tile(m, tm_cap)
    return pl.pallas_call(
        _mm_res_kernel,
        grid=(m // tm, n // tn),
        in_specs=[
            pl.BlockSpec((tm, k), lambda i, j: (i, 0)),
            pl.BlockSpec((k, tn), lambda i, j: (0, j)),
            pl.BlockSpec((tm, tn), lambda i, j: (i, j)),
        ],
        out_specs=pl.BlockSpec((tm, tn), lambda i, j: (i, j)),
        out_shape=jax.ShapeDtypeStruct((m, n), F32),
        compiler_params=_cparams(("arbitrary", "arbitrary"), 56),
        name=name,
    )(a, w, res)


ATTN_TILE_BATCH = 8


def _attn_prompt_kernel(*refs):
    grp = [refs[5 * g:5 * g + 5] for g in range(N_GROUPS)]
    o_ref, o_scr, l_scr = refs[5 * N_GROUPS:]
    first = pl.program_id(1) == 0
    row = lax.broadcasted_iota(jnp.int32, (LANES, LANES), 0)
    col = lax.broadcasted_iota(jnp.int32, (LANES, LANES), 1)
    mask_cur = col <= row
    mask_prev = col >= row
    mask_prev_first = col >= row + jnp.where(first, LANES, 0)
    scale = HEAD_DIM ** -0.5
    nq = KEYS_PER_QUERY

    def rows(dil, base, r):
        if dil == 1:
            return pl.ds(base, nq)
        return pl.ds(base + r, nq, stride=dil)

    tiles = [(g, dil, sb, r) for g, (_, dil) in enumerate(DIL_GROUPS)
             for sb in range(SUPER // (nq * dil)) for r in range(dil)]
    for t0 in range(0, len(tiles), ATTN_TILE_BATCH):
        batch = tiles[t0:t0 + ATTN_TILE_BATCH]
        scores = []
        for g, dil, sb, r in batch:
            q_ref, k_ref, v_ref, kp_ref, vp_ref = grp[g]
            span = nq * dil
            cur = rows(dil, sb * span, r)
            qt = _bf(q_ref[cur, :])
            if sb > 0:
                prev = rows(dil, (sb - 1) * span, r)
                kp, vp, mp = k_ref[prev, :], v_ref[prev, :], mask_prev
            else:
                prev = rows(dil, 0, r)
                kp, vp, mp = kp_ref[prev, :], vp_ref[prev, :], mask_prev_first
            s_c = jnp.where(mask_cur, _dot_nt(qt, _bf(k_ref[cur, :])) * scale, -jnp.inf)
            s_p = jnp.where(mp, _dot_nt(qt, _bf(kp)) * scale, -jnp.inf)
            scores.append((s_c, s_p, _bf(v_ref[cur, :]), _bf(vp)))
        probs = []
        for s_c, s_p, vc, vp in scores:
            mx = jnp.maximum(jnp.max(s_c, axis=-1, keepdims=True),
                             jnp.max(s_p, axis=-1, keepdims=True))
            p_c = jnp.exp(s_c - mx)
            p_p = jnp.exp(s_p - mx)
            den = jnp.sum(p_c, axis=-1, keepdims=True) + jnp.sum(p_p, axis=-1, keepdims=True)
            probs.append((_bf(p_c), _bf(p_p), vc, vp, den, mx))
        for (g, dil, sb, r), (p_c, p_p, vc, vp, den, mx) in zip(batch, probs):
            o = (_dot(p_c, vc) + _dot(p_p, vp)) / den
            dst = rows(dil, sb * nq * dil, r)
            o_scr[g, dst, :] = o
            l_scr[g, dst, :] = jnp.broadcast_to(mx + jnp.log(den), (nq, LANES))

    l0, l1, l2 = l_scr[0], l_scr[1], l_scr[2]
    mx = jnp.maximum(jnp.maximum(l0, l1), l2)
    e0, e1, e2 = jnp.exp(l0 - mx), jnp.exp(l1 - mx), jnp.exp(l2 - mx)
    o = (e0 * o_scr[0] + e1 * o_scr[1] + e2 * o_scr[2]) / (e0 + e1 + e2)
    o_ref[...] = _bf(o)


def _attn_prompt(proj, t):
    nsup = t // SUPER
    in_specs = []
    args = []
    for g, (_, dil) in enumerate(DIL_GROUPS):
        span = KEYS_PER_QUERY * dil
        per = SUPER // span
        cb = g * 3 * A_HEADS
        for part in range(3):
            in_specs.append(pl.BlockSpec(
                (SUPER, HEAD_DIM), lambda h, i, c=cb + part * A_HEADS: (i, c + h)))
            args.append(proj)
        for part in (1, 2):
            in_specs.append(pl.BlockSpec(
                (span, HEAD_DIM),
                lambda h, i, c=cb + part * A_HEADS, per=per: (jnp.maximum(i * per - 1, 0), c + h)))
            args.append(proj)
    return pl.pallas_call(
        _attn_prompt_kernel,
        grid=(A_HEADS, nsup),
        in_specs=in_specs,
        out_specs=pl.BlockSpec((SUPER, HEAD_DIM), lambda h, i: (i, h)),
        out_shape=jax.ShapeDtypeStruct((t, A_WIDTH), BF16),
        scratch_shapes=[pltpu.VMEM((N_GROUPS, SUPER, HEAD_DIM), F32),
                        pltpu.VMEM((N_GROUPS, SUPER, HEAD_DIM), F32)],
        compiler_params=_cparams(("arbitrary", "arbitrary"), 48),
        name="attn_prompt",
    )(*args)


def _attn_sample_kernel(qkv_ref, c0_ref, c1_ref, c2_ref, o_ref, *, bb):
    caches = (c0_ref, c1_ref, c2_ref)
    scale = HEAD_DIM ** -0.5
    sub = lax.broadcasted_iota(jnp.int32, (2 * A_HEADS, HEAD_DIM), 0)

    def on_sublanes(b, col0, first):
        out = jnp.zeros((2 * A_HEADS, HEAD_DIM), F32)
        for hh in range(A_HEADS):
            r = qkv_ref[b:b + 1, col0 + hh * HEAD_DIM:col0 + (hh + 1) * HEAD_DIM]
            out = jnp.where(sub == first + hh, r, out)
        return out

    for b in range(bb):
        outs, lses = [], []
        for g in range(N_GROUPS):
            c0 = g * 3 * A_WIDTH
            q8 = on_sublanes(b, c0, 0)
            kn8 = on_sublanes(b, c0 + A_WIDTH, 0)
            vn8 = on_sublanes(b, c0 + 2 * A_WIDTH, A_HEADS)
            kv = caches[g][b]
            s = jnp.sum(kv * q8, axis=-1, keepdims=True) * scale
            s = pltpu.roll(jnp.broadcast_to(s, kv.shape), A_HEADS, 1)
            sn = jnp.sum(kn8 * q8, axis=-1, keepdims=True) * scale
            sn = pltpu.roll(jnp.broadcast_to(sn, q8.shape), A_HEADS, 0)
            mx = jnp.maximum(jnp.max(s, axis=0), sn)
            p = jnp.exp(s - mx)
            pn = jnp.exp(sn - mx)
            den = jnp.sum(p, axis=0) + pn
            outs.append((jnp.sum(p * kv, axis=0) + pn * vn8) / den)
            lses.append(mx + jnp.log(den))
        mx = jnp.maximum(jnp.maximum(lses[0], lses[1]), lses[2])
        es = [jnp.exp(l - mx) for l in lses]
        o = (es[0] * outs[0] + es[1] * outs[1] + es[2] * outs[2]) / (es[0] + es[1] + es[2])
        for hh in range(A_HEADS):
            o_ref[b:b + 1, hh * HEAD_DIM:(hh + 1) * HEAD_DIM] = o[A_HEADS + hh:A_HEADS + hh + 1, :]


def _attn_sample(proj, caches, t, b):
    bb = SUBLANES
    in_specs = [pl.BlockSpec((bb, A_COLS), lambda i: (t // bb + i, 0))]
    args = [proj]
    for (win, dil), c in zip(DIL_GROUPS, caches):
        args.append(c.reshape(b, win // dil, dil, 2 * A_HEADS, HEAD_DIM))
        in_specs.append(pl.BlockSpec((bb, win // dil, None, 2 * A_HEADS, HEAD_DIM),
                                     lambda i: (i, 0, 0, 0, 0)))
    return pl.pallas_call(
        functools.partial(_attn_sample_kernel, bb=bb),
        grid=(b // bb,),
        in_specs=in_specs,
        out_specs=pl.BlockSpec((bb, A_WIDTH), lambda i: (i, 0)),
        out_shape=jax.ShapeDtypeStruct((b, A_WIDTH), F32),
        compiler_params=_cparams(("arbitrary",), 40),
        name="attn_sample",
    )(*args)


def _beta_decay_kernel(ba_ref, alog_ref, dtb_ref, ltri_ref, e_ref, beta_ref, gc_ref, *, cumulative):
    x = ba_ref[...]
    lane = lax.broadcasted_iota(jnp.int32, x.shape, 1)
    z = x + dtb_ref[...]
    softplus = jnp.maximum(z, 0.0) + jnp.log1p(jnp.exp(-jnp.abs(z)))
    g = -jnp.exp(alog_ref[...]) * softplus
    if cumulative:
        g = jnp.dot(ltri_ref[...], g, precision=HIGHEST, preferred_element_type=F32)
    vals = jnp.where(lane < B_HEADS, jax.nn.sigmoid(x), g)
    wide = jnp.dot(vals, e_ref[...], precision=HIGHEST, preferred_element_type=F32)
    beta_ref[...] = wide[:, :B_WIDTH]
    gc_ref[...] = wide[:, B_WIDTH:]


def _beta_decay(ba, alog_row, dtb_row, ltri, expand, row0, rows, cumulative):
    blk0 = row0 // CHUNK
    return pl.pallas_call(
        functools.partial(_beta_decay_kernel, cumulative=cumulative),
        grid=(rows // CHUNK,),
        in_specs=[
            pl.BlockSpec((CHUNK, LANES), lambda i: (blk0 + i, 0)),
            pl.BlockSpec((1, LANES), lambda i: (0, 0)),
            pl.BlockSpec((1, LANES), lambda i: (0, 0)),
            pl.BlockSpec((CHUNK, CHUNK), lambda i: (0, 0)),
            pl.BlockSpec((LANES, 2 * B_WIDTH), lambda i: (0, 0)),
        ],
        out_specs=[pl.BlockSpec((CHUNK, B_WIDTH), lambda i: (i, 0)),
                   pl.BlockSpec((CHUNK, B_WIDTH), lambda i: (i, 0))],
        out_shape=[jax.ShapeDtypeStruct((rows, B_WIDTH), F32),
                   jax.ShapeDtypeStruct((rows, B_WIDTH), F32)],
        compiler_params=_cparams(("arbitrary",), 32),
        name="beta_decay",
    )(ba, alog_row, dtb_row, ltri, expand)


def _l2norm_heads(x):
    parts = []
    for hh in range(x.shape[1] // HEAD_DIM):
        a = x[:, hh * HEAD_DIM:(hh + 1) * HEAD_DIM]
        parts.append(a * lax.rsqrt(jnp.sum(a * a, axis=-1, keepdims=True) + EPS))
    return jnp.concatenate(parts, axis=1)


def _gated_norm(o, z, nw):
    y = o * lax.rsqrt(jnp.mean(o * o, axis=-1, keepdims=True) + EPS) * nw
    return y * _silu(z)


INV_BASE = 16


assert CHUNK == LANES


def _unit_lower_inverse(a, eye, row, col):
    c = a[0].shape[0]

    def blk(size):
        sh = size.bit_length() - 1
        return (row >> sh) == (col >> sh)

    base = blk(INV_BASE)
    d = [jnp.where(base, x, 0.0) for x in a]
    tm = [eye - x for x in d]
    pw = [_bf(x) for x in d]
    for _ in range(INV_BASE.bit_length() - 2):
        pw = [_bf(_dot(x, x)) for x in pw]
        tm = [x + _dot(_bf(x), p) for x, p in zip(tm, pw)]
    size = INV_BASE
    while size < c:
        sel = jnp.logical_and(blk(2 * size), jnp.logical_not(blk(size)))
        tm_bf = [_bf(x) for x in tm]
        inner = [_bf(_dot(_bf(jnp.where(sel, x, 0.0)), y)) for x, y in zip(a, tm_bf)]
        tm = [x - _dot(y, z) for x, y, z in zip(tm, tm_bf, inner)]
        size *= 2
    return tm


def _delta_prompt_kernel(x_ref, z_ref, ba_ref, alog_ref, dtb_ref, ltri_ref, cw_ref, nw_ref,
                         o_ref, sfin_ref, s_scr, buf):
    j = pl.program_id(0)
    c = CHUNK
    halo = SUBLANES
    heads = range(B_HEADS)

    @pl.when(j == 0)
    def _():
        s_scr[...] = jnp.zeros_like(s_scr)
        buf[0:halo, :] = jnp.zeros((halo, buf.shape[1]), F32)

    row = lax.broadcasted_iota(jnp.int32, (c, c), 0)
    col = lax.broadcasted_iota(jnp.int32, (c, c), 1)
    tril = col <= row
    strict = col < row
    eye = (col == row).astype(F32)

    logits = ba_ref[...]
    zz = logits + dtb_ref[...]
    softplus = jnp.maximum(zz, 0.0) + jnp.log1p(jnp.exp(-jnp.abs(zz)))
    g_cum = jnp.dot(ltri_ref[...], -jnp.exp(alog_ref[...]) * softplus,
                    precision=HIGHEST, preferred_element_type=F32)
    lane = lax.broadcasted_iota(jnp.int32, logits.shape, 1)
    per_head = jnp.where(lane < B_HEADS, jax.nn.sigmoid(logits), g_cum).T
    gc_t = [jnp.broadcast_to(per_head[B_HEADS + hh:B_HEADS + hh + 1, :], (c, c)) for hh in heads]
    gc = [x.T for x in gc_t]
    beta = [jnp.broadcast_to(per_head[hh:hh + 1, :], (c, c)).T for hh in heads]

    buf[halo:halo + c, :] = x_ref[...]

    def conv(c0):
        cs = slice(c0, c0 + LANES)
        acc = buf[pl.ds(halo, c), cs] * cw_ref[B_CONV - 1:B_CONV, cs]
        for s in range(1, B_CONV):
            acc = acc + buf[pl.ds(halo - s, c), cs] * cw_ref[B_CONV - 1 - s:B_CONV - s, cs]
        return _silu(acc)

    def l2n(a):
        return a * lax.rsqrt(jnp.sum(a * a, axis=-1, keepdims=True) + EPS)

    q = [l2n(conv(hh * LANES)) * (B_DK ** -0.5) for hh in heads]
    k = [l2n(conv(B_WIDTH + hh * LANES)) for hh in heads]
    v = [conv(2 * B_WIDTH + hh * LANES) for hh in heads]
    buf[0:halo, :] = buf[c:c + halo, :]

    k_bf = [_bf(x) for x in k]
    kb = [x * bb for x, bb in zip(k, beta)]
    kq = [_dot_nt(jnp.concatenate([_bf(x), _bf(y)], axis=0), kk) for x, y, kk in zip(kb, q, k_bf)]
    decay = [jnp.exp(jnp.where(tril, g - gt, -jnp.inf)) for g, gt in zip(gc, gc_t)]
    a = [jnp.where(strict, x[:c] * dd, 0.0) for x, dd in zip(kq, decay)]
    qk_bf = [_bf(x[c:] * dd) for x, dd in zip(kq, decay)]
    tm = _unit_lower_inverse(a, eye, row, col)
    e_g = [jnp.exp(g) for g in gc]
    uw = [_dot(_bf(t), jnp.concatenate([_bf(vv * bb), _bf(x * eg)], axis=1))
          for t, vv, bb, x, eg in zip(tm, v, beta, kb, e_g)]
    s = [s_scr[hh] for hh in heads]
    s_bf = [_bf(x) for x in s]
    ws = [_dot(jnp.concatenate([_bf(x[:, LANES:]), _bf(qq * eg)], axis=0), sb)
          for x, qq, eg, sb in zip(uw, q, e_g, s_bf)]
    v_bf = [_bf(x[:, :LANES] - y[:c]) for x, y in zip(uw, ws)]
    o = [y[c:] + _dot(qq, vv) for y, qq, vv in zip(ws, qk_bf, v_bf)]
    g_last = [g[c - 1:c, :] for g in gc]
    kd_bf = [_bf(kk * jnp.exp(gl - g)) for kk, gl, g in zip(k, g_last, gc)]
    for hh in heads:
        sl = slice(hh * LANES, (hh + 1) * LANES)
        s_scr[hh] = s[hh] * jnp.exp(g_last[hh]) + _dot_tn(kd_bf[hh], v_bf[hh])
        o_ref[:, sl] = _gated_norm(o[hh], z_ref[:, sl], nw_ref[...]).astype(BF16)

    @pl.when(j == pl.num_programs(0) - 1)
    def _():
        sfin_ref[...] = s_scr[...]


def _delta_prompt(proj, ba, alog_row, dtb_row, ltri, cw, nw, t, x_col0, z_col0):
    c = CHUNK
    const = lambda shape: pl.BlockSpec(shape, lambda j: (0,) * len(shape))
    return pl.pallas_call(
        _delta_prompt_kernel,
        grid=(t // c,),
        in_specs=[
            pl.BlockSpec((pl.Element(c), pl.Element(B_CONV_COLS)),
                         lambda j: (pl.multiple_of(j * c, c), x_col0)),
            pl.BlockSpec((pl.Element(c), pl.Element(B_WIDTH)),
                         lambda j: (pl.multiple_of(j * c, c), z_col0)),
            pl.BlockSpec((c, LANES), lambda j: (j, 0)),
            const((1, LANES)), const((1, LANES)), const((c, c)), const((B_CONV, B_CONV_COLS)),
            const((1, LANES)),
        ],
        out_specs=[pl.BlockSpec((c, B_WIDTH), lambda j: (j, 0)),
                   const((B_HEADS, B_DK, B_DV))],
        out_shape=[jax.ShapeDtypeStruct((t, B_WIDTH), BF16),
                   jax.ShapeDtypeStruct((B_HEADS, B_DK, B_DV), F32)],
        scratch_shapes=[pltpu.VMEM((B_HEADS, B_DK, B_DV), F32),
                        pltpu.VMEM((c + SUBLANES, B_CONV_COLS), F32)],
        compiler_params=_cparams(("arbitrary",), 40),
        name="delta_prompt",
    )(proj, proj, ba, alog_row, dtb_row, ltri, cw, nw)


def _conv_sample_kernel(x_ref, s0_ref, s1_ref, s2_ref, cw_ref, o_ref):
    c = pl.program_id(0)
    acc = (s0_ref[...] * cw_ref[0:1, :] + s1_ref[...] * cw_ref[1:2, :]
           + s2_ref[...] * cw_ref[2:3, :] + x_ref[...] * cw_ref[3:4, :])
    y = _silu(acc)
    n_qk = 2 * B_HEADS // HEAD_GROUP

    @pl.when(c < n_qk // 2)
    def _():
        o_ref[...] = _l2norm_heads(y) * (B_DK ** -0.5)

    @pl.when(jnp.logical_and(c >= n_qk // 2, c < n_qk))
    def _():
        o_ref[...] = _l2norm_heads(y)

    @pl.when(c >= n_qk)
    def _():
        o_ref[...] = y


def _conv_sample(proj, state, cw, t, b, col0):
    ncb = B_CONV_COLS // COL_BLOCK
    cb0 = col0 // COL_BLOCK
    rb0 = t // b
    sspec = pl.BlockSpec((b, COL_BLOCK), lambda c: (0, c))
    return pl.pallas_call(
        _conv_sample_kernel,
        grid=(ncb,),
        in_specs=[pl.BlockSpec((b, COL_BLOCK), lambda c: (rb0, cb0 + c)), sspec, sspec, sspec,
                  pl.BlockSpec((B_CONV, COL_BLOCK), lambda c: (0, c))],
        out_specs=pl.BlockSpec((b, COL_BLOCK), lambda c: (0, c)),
        out_shape=jax.ShapeDtypeStruct((b, B_CONV_COLS), F32),
        compiler_params=_cparams(("arbitrary",), 32),
        name="conv_sample",
    )(proj, state[:, 0], state[:, 1], state[:, 2], cw)


def _delta_sample_kernel(q_ref, k_ref, v_ref, beta_ref, g_ref, z_ref, nw_ref, s_ref, o_ref, snew_ref, *, bb):
    row = lax.broadcasted_iota(jnp.int32, (LANES, LANES), 0)
    col = lax.broadcasted_iota(jnp.int32, (LANES, LANES), 1)
    eye = col == row

    def column(r):
        return jnp.sum(jnp.where(eye, r, 0.0), axis=-1, keepdims=True)

    units = [(b, hh, slice(hh * LANES, (hh + 1) * LANES)) for b in range(bb) for hh in range(HEAD_GROUP)]
    rows = [(q_ref[b:b + 1, sl], k_ref[b:b + 1, sl]) for b, _, sl in units]
    cols = [(column(q), column(k)) for q, k in rows]
    qk = [jnp.sum(q * k, axis=-1, keepdims=True) for q, k in rows]
    reads = [(jnp.sum(s_ref[b, hh] * qc, axis=0, keepdims=True),
              jnp.sum(s_ref[b, hh] * kc, axis=0, keepdims=True))
             for (b, hh, _), (qc, kc) in zip(units, cols)]
    outs = []
    for (b, hh, sl), (_, k_col), (qs, ks), qk_u in zip(units, cols, reads, qk):
        e_g = jnp.exp(g_ref[b:b + 1, sl])
        v_new = beta_ref[b:b + 1, sl] * (v_ref[b:b + 1, sl] - e_g * ks)
        outs.append(e_g * qs + qk_u * v_new)
        snew_ref[b, hh] = s_ref[b, hh] * e_g + k_col * v_new
    for (b, _, sl), o in zip(units, outs):
        o_ref[b:b + 1, sl] = _gated_norm(o, z_ref[b:b + 1, sl], nw_ref[...])


def _delta_sample(qkv, beta, g, proj, nw, state, t, b, z_col0):
    bb = SUBLANES
    hgs = B_HEADS // HEAD_GROUP
    zb0 = z_col0 // COL_BLOCK
    rb0 = t // bb
    blk = lambda off: pl.BlockSpec((bb, COL_BLOCK), lambda i, h, off=off: (i, off + h))
    sspec = pl.BlockSpec((bb, HEAD_GROUP, B_DK, B_DV), lambda i, h: (i, h, 0, 0))
    return pl.pallas_call(
        functools.partial(_delta_sample_kernel, bb=bb),
        grid=(b // bb, hgs),
        in_specs=[blk(0), blk(hgs), blk(2 * hgs), blk(0), blk(0),
                  pl.BlockSpec((bb, COL_BLOCK), lambda i, h: (rb0 + i, zb0 + h)),
                  pl.BlockSpec((1, LANES), lambda i, h: (0, 0)), sspec],
        out_specs=[pl.BlockSpec((bb, COL_BLOCK), lambda i, h: (i, h)), sspec],
        out_shape=[jax.ShapeDtypeStruct((b, B_WIDTH), F32),
                   jax.ShapeDtypeStruct(state.shape, F32)],
        compiler_params=_cparams(("arbitrary", "arbitrary"), 32),
        name="delta_sample",
    )(qkv, qkv, qkv, beta, g, proj, nw, state)


def _ffn_kernel(h_ref, hs_ref, wg_ref, wu_ref, cw_ref, s0_ref, s1_ref, act_ref, tail_ref, gs_ref,
                wg_bf, wu_bf, buf, *, tm, bs):
    i = pl.program_id(1)
    halo = SUBLANES

    @pl.when(i == 0)
    def _():
        wg_bf[...] = _bf(wg_ref[...])
        wu_bf[...] = _bf(wu_ref[...])
        buf[0:halo, :] = jnp.zeros((halo, buf.shape[1]), F32)
        h = hs_ref[...]
        g = _dot(h, wg_bf[...])
        gs_ref[...] = g
        acc = s0_ref[...] * cw_ref[0:1, :] + s1_ref[...] * cw_ref[1:2, :] + g * cw_ref[2:3, :]
        act_ref[0:bs, :] = _bf(_silu(acc) * _dot(h, wu_bf[...]))

    @pl.when(i > 0)
    def _():
        h = h_ref[...]
        buf[halo:halo + tm, :] = _dot(h, wg_bf[...])
        acc = buf[pl.ds(halo, tm), :] * cw_ref[FFN_CONV - 1:FFN_CONV, :]
        for s in range(1, FFN_CONV):
            acc = acc + buf[pl.ds(halo - s, tm), :] * cw_ref[FFN_CONV - 1 - s:FFN_CONV - s, :]
        tail = buf[tm:tm + halo, :]
        buf[0:halo, :] = tail
        tail_ref[...] = tail
        act_ref[...] = _bf(_silu(acc) * _dot(h, wu_bf[...]))


def _ffn(h2, wg, wu, cw, state, t, b):
    m, d = h2.shape
    f = wg.shape[1]
    tf = 256
    tm = _row_tile(t, 1024)
    npt = t // tm
    assert b <= tm and m == t + b and t % b == 0
    sspec = pl.BlockSpec((b, tf), lambda c, i: (0, c))
    return pl.pallas_call(
        functools.partial(_ffn_kernel, tm=tm, bs=b),
        grid=(f // tf, npt + 1),
        in_specs=[
            pl.BlockSpec((tm, d), lambda c, i: (jnp.maximum(i - 1, 0), 0)),
            pl.BlockSpec((b, d), lambda c, i: (t // b, 0)),
            pl.BlockSpec((d, tf), lambda c, i: (0, c)),
            pl.BlockSpec((d, tf), lambda c, i: (0, c)),
            pl.BlockSpec((FFN_CONV, tf), lambda c, i: (0, c)),
            sspec, sspec,
        ],
        out_specs=[pl.BlockSpec((tm, tf), lambda c, i: (jnp.where(i == 0, npt, i - 1), c)),
                   pl.BlockSpec((SUBLANES, tf), lambda c, i: (0, c)),
                   pl.BlockSpec((b, tf), lambda c, i: (0, c))],
        out_shape=[jax.ShapeDtypeStruct((m, f), BF16),
                   jax.ShapeDtypeStruct((SUBLANES, f), F32),
                   jax.ShapeDtypeStruct((b, f), F32)],
        scratch_shapes=[pltpu.VMEM((d, tf), BF16), pltpu.VMEM((d, tf), BF16),
                        pltpu.VMEM((tm + SUBLANES, tf), F32)],
        compiler_params=_cparams(("arbitrary", "arbitrary"), 56),
        name="ffn",
    )(h2, h2, wg, wu, cw, state[:, 0], state[:, 1])


def _rope_tables(t, b):
    inv = ROPE_THETA ** (-jnp.arange(0, ROT_DIM, 2, dtype=F32) / ROT_DIM)
    pos = jnp.concatenate([jnp.arange(t, dtype=jnp.int32),
                           jnp.full((b,), PAST_LEN, jnp.int32)]).astype(F32)
    ang = pos[:, None] * inv[None, :]
    cos, sin = jnp.cos(ang), jnp.sin(ang)
    m = t + b
    pad = LANES - ROT_DIM
    cos_t = jnp.concatenate([cos, cos, jnp.ones((m, pad), F32)], axis=1)
    sin_t = jnp.concatenate([-sin, sin, jnp.zeros((m, pad), F32)], axis=1)
    return cos_t, sin_t


def kernel(x_prompt, x_sample, cache_kv_w128, cache_kv_w512, cache_kv_w2048, state_conv_qkv, state_delta,
           state_ffn_conv, norm_mix, w_in, conv_qkv, a_log, dt_bias, delta_norm, w_a_out, w_b_out, w_out,
           norm_ffn, w_gate, ffn_conv, w_up, w_down, norm_final):
    assert norm_mix.shape[0] == 1, "single layer only"
    assert x_prompt.shape[0] == 1 and x_sample.shape[1] == 1
    t, d = x_prompt.shape[1], x_prompt.shape[2]
    b = x_sample.shape[0]
    assert t % SUPER == 0 and b == ROW_BLOCK
    caches = (cache_kv_w128[0], cache_kv_w512[0], cache_kv_w2048[0])
    for (win, _), c in zip(DIL_GROUPS, caches):
        assert c.shape[1] == win, "cache must hold a full window"

    off_bqkv = A_COLS
    off_bz = off_bqkv + B_CONV_COLS
    off_ba = off_bz + B_WIDTH
    off_gate = off_ba + 2 * B_HEADS
    assert w_in.shape[2] == off_gate + 2 * d
    w_in_t = jnp.swapaxes(w_in[0], 0, 1)
    w_main = _bf(w_in_t)
    w_ba = jnp.pad(w_in_t[off_ba:off_gate], ((0, LANES - 2 * B_HEADS), (0, 0)))
    w_ba_hi = _bf(w_ba)
    w_ba = jnp.concatenate([w_ba_hi, _bf(w_ba - w_ba_hi.astype(F32))], axis=0)
    gate_col0 = off_ba

    xp = x_prompt.reshape(t, d)
    xs = x_sample.reshape(b, d)
    h, x_cat, ba = _norm_in(xp, xs, norm_mix, w_ba)
    cos_t, sin_t = _rope_tables(t, b)
    proj = _proj(h, w_main, cos_t, sin_t, gate_col0, off_gate)

    oa_p = _attn_prompt(proj, t)
    oa_s = _attn_sample(proj, caches, t, b)
    o_a = jnp.concatenate([oa_p, _bf(oa_s)], axis=0)

    alog_row = jnp.pad(a_log[0], (B_HEADS, LANES - 2 * B_HEADS)).reshape(1, LANES)
    dtb_row = jnp.pad(dt_bias[0], (B_HEADS, LANES - 2 * B_HEADS)).reshape(1, LANES)
    ltri = jnp.tril(jnp.ones((CHUNK, CHUNK), F32))
    head_of_col = jnp.arange(2 * B_WIDTH, dtype=jnp.int32) // LANES
    expand = (jnp.arange(LANES, dtype=jnp.int32)[:, None] == head_of_col[None, :]).astype(F32)
    cw_qkv = conv_qkv[0]
    ob_p, delta_p = _delta_prompt(proj, ba, alog_row, dtb_row, ltri, cw_qkv, delta_norm, t, off_bqkv, off_bz)
    beta_s, g_s = _beta_decay(ba, alog_row, dtb_row, ltri, expand, t, b, False)
    qkv_s = _conv_sample(proj, state_conv_qkv[0], cw_qkv, t, b, off_bqkv)
    ob_s, delta_s = _delta_sample(qkv_s, beta_s, g_s, proj, delta_norm, state_delta[0], t, b, off_bz)
    o_b = jnp.concatenate([ob_p, _bf(ob_s)], axis=0)

    merged = _merge(o_a, o_b, _bf(w_a_out[0]), _bf(w_b_out[0]), proj, gate_col0)
    x1 = _mm_res(merged, _bf(w_out[0]), x_cat, ROW_TILE_CAP, 512, "out_proj")

    h2 = _norm_mid(x1, norm_ffn)
    act, g_tail, g_smp = _ffn(h2, w_gate[0], w_up[0], ffn_conv[0], state_ffn_conv[0], t, b)
    x2 = _mm_res(act, _bf(w_down[0]), x1, 640, 256, "down_proj")
    y_p, y_s = _norm_out(x2, norm_final.reshape(1, d), t, b)

    kv_p, kv_s = [], []
    for g, (win, _) in enumerate(DIL_GROUPS):
        c0 = g * 3 * A_WIDTH + A_WIDTH
        keep = min(win, t)
        kv_p.append(proj[t - keep:t, c0:c0 + 2 * A_WIDTH].reshape(1, 1, keep, 2, A_HEADS, HEAD_DIM))
        kv_s.append(proj[t:, c0:c0 + 2 * A_WIDTH].reshape(1, b, 1, 2, A_HEADS, HEAD_DIM))
    conv_p = proj[t - (B_CONV - 1):t, off_bqkv:off_bz].reshape(1, 1, B_CONV - 1, B_CONV_COLS)
    conv_s = jnp.concatenate([state_conv_qkv[0][:, 1:], proj[t:, off_bqkv:off_bz][:, None]], axis=1)[None]
    ffn_p = g_tail[SUBLANES - (FFN_CONV - 1):].reshape(1, 1, FFN_CONV - 1, -1)
    ffn_s = jnp.concatenate([state_ffn_conv[0][:, 1:], g_smp[:, None]], axis=1)[None]
    return (y_p.reshape(1, t, d), y_s.reshape(b, 1, d), kv_p[0], kv_p[1], kv_p[2],
            conv_p, delta_p.reshape(1, 1, B_HEADS, B_DK, B_DV), ffn_p,
            kv_s[0], kv_s[1], kv_s[2], conv_s, delta_s[None], ffn_s)
```

```python
import functools

import jax
import jax.numpy as jnp
from jax import lax
from jax.experimental import pallas as pl
from jax.experimental.pallas import tpu as pltpu

F32 = jnp.float32
BF16 = jnp.bfloat16
HIGHEST = lax.Precision.HIGHEST

HEAD_DIM = 128
ROT_DIM = HEAD_DIM // 4
ROT_HALF = ROT_DIM // 2
ROPE_THETA = 500000.0
DIL_GROUPS = ((128, 1), (512, 4), (2048, 16))
N_GROUPS = len(DIL_GROUPS)
A_HEADS = 4
A_WIDTH = A_HEADS * HEAD_DIM
A_COLS = N_GROUPS * 3 * A_WIDTH
KEYS_PER_QUERY = 128
B_HEADS = 16
B_DK = 128
B_DV = 128
B_WIDTH = B_HEADS * B_DK
B_CONV = 4
B_CONV_COLS = 3 * B_WIDTH
FFN_CONV = 3
EPS = 1e-6
PAST_LEN = 2048

LANES = 128
SUBLANES = 8
ROW_BLOCK = 128
SUPER = 2048
CHUNK = 128
HEAD_GROUP = 4
COL_BLOCK = HEAD_GROUP * LANES
VMEM_CAP_MB = 60
ROW_PARTS = 4
ROW_TILE_CAP = 1664


def _cparams(sem, vmem_mb):
    return pltpu.CompilerParams(dimension_semantics=sem,
                                vmem_limit_bytes=int(min(vmem_mb, VMEM_CAP_MB) * 2 ** 20))


def _bf(x):
    return x.astype(BF16)


def _dot(a, b):
    return jnp.dot(a, b, preferred_element_type=F32)


def _dot_nt(a, b):
    return lax.dot_general(a, b, (((1,), (1,)), ((), ())), preferred_element_type=F32)


def _dot_tn(a, b):
    return lax.dot_general(a, b, (((0,), (0,)), ((), ())), preferred_element_type=F32)


def _silu(x):
    return x * jax.nn.sigmoid(x)


NORM_ROWS = 256


def _norm_in_kernel(xp_ref, xs_ref, nw_ref, wba_ref, h_ref, xcat_ref, ba_ref, *, n_prompt_blocks, bs):
    i = pl.program_id(0)

    def body(x, rows):
        y = x * lax.rsqrt(jnp.mean(x * x, axis=-1, keepdims=True) + EPS) * nw_ref[...]
        y_hi = _bf(y)
        y_lo = _bf(y - y_hi.astype(F32))
        h_ref[rows, :] = y_hi
        xcat_ref[rows, :] = x
        hi = _dot_nt(y_hi, wba_ref[...])
        ba_ref[rows, :] = hi[:, :LANES] + hi[:, LANES:] + _dot_nt(y_lo, wba_ref[0:LANES, :])

    @pl.when(i < n_prompt_blocks)
    def _():
        body(xp_ref[...], slice(None))

    @pl.when(i == n_prompt_blocks)
    def _():
        body(xs_ref[...], slice(0, bs))


def _norm_in(xp, xs, nw, wba):
    t, d = xp.shape
    b = xs.shape[0]
    rb = NORM_ROWS
    npb = t // rb
    m = t + b
    return pl.pallas_call(
        functools.partial(_norm_in_kernel, n_prompt_blocks=npb, bs=b),
        grid=(npb + 1,),
        in_specs=[
            pl.BlockSpec((rb, d), lambda i: (jnp.minimum(i, npb - 1), 0)),
            pl.BlockSpec((b, d), lambda i: (0, 0)),
            pl.BlockSpec((1, d), lambda i: (0, 0)),
            pl.BlockSpec((2 * LANES, d), lambda i: (0, 0)),
        ],
        out_specs=[
            pl.BlockSpec((rb, d), lambda i: (i, 0)),
            pl.BlockSpec((rb, d), lambda i: (i, 0)),
            pl.BlockSpec((rb, LANES), lambda i: (i, 0)),
        ],
        out_shape=[
            jax.ShapeDtypeStruct((m, d), BF16),
            jax.ShapeDtypeStruct((m, d), F32),
            jax.ShapeDtypeStruct((m, LANES), F32),
        ],
        compiler_params=_cparams(("arbitrary",), 40),
        name="norm_in",
    )(xp, xs, nw, wba)


def _norm_mid_kernel(x_ref, nw_ref, h_ref):
    x = x_ref[...]
    y = x * lax.rsqrt(jnp.mean(x * x, axis=-1, keepdims=True) + EPS) * nw_ref[...]
    h_ref[...] = _bf(y)


def _norm_mid(x, nw):
    m, d = x.shape
    rb = _row_tile(m, 640)
    return pl.pallas_call(
        _norm_mid_kernel,
        grid=(m // rb,),
        in_specs=[pl.BlockSpec((rb, d), lambda i: (i, 0)),
                  pl.BlockSpec((1, d), lambda i: (0, 0))],
        out_specs=pl.BlockSpec((rb, d), lambda i: (i, 0)),
        out_shape=jax.ShapeDtypeStruct((m, d), BF16),
        compiler_params=_cparams(("arbitrary",), 40),
        name="norm_mid",
    )(x, nw)


def _norm_out_kernel(x_ref, nw_ref, yp_ref, ys_ref, *, n_prompt_blocks, bs):
    i = pl.program_id(0)

    def norm(x):
        return x * lax.rsqrt(jnp.mean(x * x, axis=-1, keepdims=True) + EPS) * nw_ref[...]

    @pl.when(i < n_prompt_blocks)
    def _():
        yp_ref[...] = norm(x_ref[...])

    @pl.when(i == n_prompt_blocks)
    def _():
        ys_ref[...] = norm(x_ref[0:bs, :])


def _norm_out(x, nw, t, b):
    d = x.shape[1]
    rb = NORM_ROWS
    npb = t // rb
    return pl.pallas_call(
        functools.partial(_norm_out_kernel, n_prompt_blocks=npb, bs=b),
        grid=(npb + 1,),
        in_specs=[pl.BlockSpec((rb, d), lambda i: (i, 0)),
                  pl.BlockSpec((1, d), lambda i: (0, 0))],
        out_specs=[
            pl.BlockSpec((rb, d), lambda i: (jnp.minimum(i, npb - 1), 0)),
            pl.BlockSpec((b, d), lambda i: (0, 0)),
        ],
        out_shape=[jax.ShapeDtypeStruct((t, d), F32), jax.ShapeDtypeStruct((b, d), F32)],
        compiler_params=_cparams(("arbitrary",), 32),
        name="norm_out",
    )(x, nw)


def _row_tile(m, cap):
    best = 16
    for c in range(16, cap + 1, 16):
        if m % c == 0:
            best = c
    return best


def _proj_kernel(h_ref, w_ref, cos_ref, sin_ref, o_ref, *, n_attn_tiles, gate_tile0):
    j = pl.program_id(1)
    is_rope = jnp.logical_and(j < n_attn_tiles, j % 3 != 2)
    is_gate = j >= gate_tile0

    rows = h_ref.shape[0] // ROW_PARTS
    parts = [slice(p * rows, (p + 1) * rows) for p in range(ROW_PARTS)]

    def products():
        return [_dot_nt(h_ref[sl, :], w_ref[...]) for sl in parts]

    @pl.when(is_rope)
    def _():
        for sl, acc in zip(parts, products()):
            cos = cos_ref[sl, :]
            sin = sin_ref[sl, :]
            lane = lax.broadcasted_iota(jnp.int32, cos.shape, 1)
            for hh in range(A_HEADS):
                a = acc[:, hh * HEAD_DIM:(hh + 1) * HEAD_DIM]
                partner = jnp.where(lane < ROT_HALF,
                                    pltpu.roll(a, HEAD_DIM - ROT_HALF, 1),
                                    pltpu.roll(a, ROT_HALF, 1))
                o_ref[sl, hh * HEAD_DIM:(hh + 1) * HEAD_DIM] = a * cos + partner * sin

    @pl.when(is_gate)
    def _():
        for sl, acc in zip(parts, products()):
            o_ref[sl, :] = jax.nn.sigmoid(acc)

    @pl.when(jnp.logical_not(jnp.logical_or(is_rope, is_gate)))
    def _():
        for sl, acc in zip(parts, products()):
            o_ref[sl, :] = acc


def _proj(h, w, cos, sin, gate_col0, gate_row0):
    m, k = h.shape
    skip = gate_row0 - gate_col0
    n = w.shape[0] - skip
    tn = A_WIDTH
    tm = _row_tile(m, ROW_TILE_CAP)
    gate_tile0 = gate_col0 // tn
    return pl.pallas_call(
        functools.partial(_proj_kernel, n_attn_tiles=A_COLS // tn, gate_tile0=gate_tile0),
        grid=(m // tm, n // tn),
        in_specs=[
            pl.BlockSpec((tm, k), lambda i, j: (i, 0)),
            pl.BlockSpec((pl.Element(tn), pl.Element(k)),
                         lambda i, j: (pl.multiple_of(j * tn + jnp.where(j >= gate_tile0, skip, 0),
                                                      2 * SUBLANES), 0)),
            pl.BlockSpec((tm, LANES), lambda i, j: (i, 0)),
            pl.BlockSpec((tm, LANES), lambda i, j: (i, 0)),
        ],
        out_specs=pl.BlockSpec((tm, tn), lambda i, j: (i, j)),
        out_shape=jax.ShapeDtypeStruct((m, n), F32),
        compiler_params=_cparams(("arbitrary", "arbitrary"), 54),
        name="proj_in",
    )(h, w, cos, sin)


def _merge_kernel(oa_ref, ob_ref, wa_ref, wb_ref, ga_ref, gb_ref, o_ref):
    a = _dot(oa_ref[...], wa_ref[...])
    b = _dot(ob_ref[...], wb_ref[...])
    o_ref[...] = _bf(ga_ref[...] * a + gb_ref[...] * b)


def _merge(oa, ob, wa, wb, proj, gate_col0):
    m = oa.shape[0]
    n = wa.shape[1]
    tn = COL_BLOCK
    tm = _row_tile(m, ROW_TILE_CAP)
    ga0 = gate_col0 // tn
    gb0 = (gate_col0 + n) // tn
    return pl.pallas_call(
        _merge_kernel,
        grid=(m // tm, n // tn),
        in_specs=[
            pl.BlockSpec((tm, oa.shape[1]), lambda i, j: (i, 0)),
            pl.BlockSpec((tm, ob.shape[1]), lambda i, j: (i, 0)),
            pl.BlockSpec((wa.shape[0], tn), lambda i, j: (0, j)),
            pl.BlockSpec((wb.shape[0], tn), lambda i, j: (0, j)),
            pl.BlockSpec((tm, tn), lambda i, j: (i, ga0 + j)),
            pl.BlockSpec((tm, tn), lambda i, j: (i, gb0 + j)),
        ],
        out_specs=pl.BlockSpec((tm, tn), lambda i, j: (i, j)),
        out_shape=jax.ShapeDtypeStruct((m, n), BF16),
        compiler_params=_cparams(("arbitrary", "arbitrary"), 48),
        name="merge",
    )(oa, ob, wa, wb, proj, proj)


def _mm_res_kernel(a_ref, w_ref, r_ref, o_ref):
    o_ref[...] = r_ref[...] + _dot(a_ref[...], w_ref[...])


def _mm_res(a, w, res, tm_cap, tn, name):
    m, k = a.shape
    n = w.shape[1]
    tm = _row_tile(m, tm_cap)
    return pl.pallas_call(
        _mm_res_kernel,
        grid=(m // tm, n // tn),
        in_specs=[
            pl.BlockSpec((tm, k), lambda i, j: (i, 0)),
            pl.BlockSpec((k, tn), lambda i, j: (0, j)),
            pl.BlockSpec((tm, tn), lambda i, j: (i, j)),
        ],
        out_specs=pl.BlockSpec((tm, tn), lambda i, j: (i, j)),
        out_shape=jax.ShapeDtypeStruct((m, n), F32),
        compiler_params=_cparams(("arbitrary", "arbitrary"), 56),
        name=name,
    )(a, w, res)


ATTN_TILE_BATCH = 8


def _attn_prompt_kernel(*refs):
    grp = [refs[5 * g:5 * g + 5] for g in range(N_GROUPS)]
    o_ref, o_scr, l_scr = refs[5 * N_GROUPS:]
    first = pl.program_id(1) == 0
    row = lax.broadcasted_iota(jnp.int32, (LANES, LANES), 0)
    col = lax.broadcasted_iota(jnp.int32, (LANES, LANES), 1)
    mask_cur = col <= row
    mask_prev = col >= row
    mask_prev_first = col >= row + jnp.where(first, LANES, 0)
    scale = HEAD_DIM ** -0.5
    nq = KEYS_PER_QUERY

    def rows(dil, base, r):
        if dil == 1:
            return pl.ds(base, nq)
        return pl.ds(base + r, nq, stride=dil)

    tiles = [(g, dil, sb, r) for g, (_, dil) in enumerate(DIL_GROUPS)
             for sb in range(SUPER // (nq * dil)) for r in range(dil)]
    for t0 in range(0, len(tiles), ATTN_TILE_BATCH):
        batch = tiles[t0:t0 + ATTN_TILE_BATCH]
        scores = []
        for g, dil, sb, r in batch:
            q_ref, k_ref, v_ref, kp_ref, vp_ref = grp[g]
            span = nq * dil
            cur = rows(dil, sb * span, r)
            qt = _bf(q_ref[cur, :])
            if sb > 0:
                prev = rows(dil, (sb - 1) * span, r)
                kp, vp, mp = k_ref[prev, :], v_ref[prev, :], mask_prev
            else:
                prev = rows(dil, 0, r)
                kp, vp, mp = kp_ref[prev, :], vp_ref[prev, :], mask_prev_first
            s_c = jnp.where(mask_cur, _dot_nt(qt, _bf(k_ref[cur, :])) * scale, -jnp.inf)
            s_p = jnp.where(mp, _dot_nt(qt, _bf(kp)) * scale, -jnp.inf)
            scores.append((s_c, s_p, _bf(v_ref[cur, :]), _bf(vp)))
        probs = []
        for s_c, s_p, vc, vp in scores:
            mx = jnp.maximum(jnp.max(s_c, axis=-1, keepdims=True),
                             jnp.max(s_p, axis=-1, keepdims=True))
            p_c = jnp.exp(s_c - mx)
            p_p = jnp.exp(s_p - mx)
            den = jnp.sum(p_c, axis=-1, keepdims=True) + jnp.sum(p_p, axis=-1, keepdims=True)
            probs.append((_bf(p_c), _bf(p_p), vc, vp, den, mx))
        for (g, dil, sb, r), (p_c, p_p, vc, vp, den, mx) in zip(batch, probs):
            o = (_dot(p_c, vc) + _dot(p_p, vp)) / den
            dst = rows(dil, sb * nq * dil, r)
            o_scr[g, dst, :] = o
            l_scr[g, dst, :] = jnp.broadcast_to(mx + jnp.log(den), (nq, LANES))

    l0, l1, l2 = l_scr[0], l_scr[1], l_scr[2]
    mx = jnp.maximum(jnp.maximum(l0, l1), l2)
    e0, e1, e2 = jnp.exp(l0 - mx), jnp.exp(l1 - mx), jnp.exp(l2 - mx)
    o = (e0 * o_scr[0] + e1 * o_scr[1] + e2 * o_scr[2]) / (e0 + e1 + e2)
    o_ref[...] = _bf(o)


def _attn_prompt(proj, t):
    nsup = t // SUPER
    in_specs = []
    args = []
    for g, (_, dil) in enumerate(DIL_GROUPS):
        span = KEYS_PER_QUERY * dil
        per = SUPER // span
        cb = g * 3 * A_HEADS
        for part in range(3):
            in_specs.append(pl.BlockSpec(
                (SUPER, HEAD_DIM), lambda h, i, c=cb + part * A_HEADS: (i, c + h)))
            args.append(proj)
        for part in (1, 2):
            in_specs.append(pl.BlockSpec(
                (span, HEAD_DIM),
                lambda h, i, c=cb + part * A_HEADS, per=per: (jnp.maximum(i * per - 1, 0), c + h)))
            args.append(proj)
    return pl.pallas_call(
        _attn_prompt_kernel,
        grid=(A_HEADS, nsup),
        in_specs=in_specs,
        out_specs=pl.BlockSpec((SUPER, HEAD_DIM), lambda h, i: (i, h)),
        out_shape=jax.ShapeDtypeStruct((t, A_WIDTH), BF16),
        scratch_shapes=[pltpu.VMEM((N_GROUPS, SUPER, HEAD_DIM), F32),
                        pltpu.VMEM((N_GROUPS, SUPER, HEAD_DIM), F32)],
        compiler_params=_cparams(("arbitrary", "arbitrary"), 48),
        name="attn_prompt",
    )(*args)


def _attn_sample_kernel(qkv_ref, c0_ref, c1_ref, c2_ref, o_ref, *, bb):
    caches = (c0_ref, c1_ref, c2_ref)
    scale = HEAD_DIM ** -0.5
    sub = lax.broadcasted_iota(jnp.int32, (2 * A_HEADS, HEAD_DIM), 0)

    def on_sublanes(b, col0, first):
        out = jnp.zeros((2 * A_HEADS, HEAD_DIM), F32)
        for hh in range(A_HEADS):
            r = qkv_ref[b:b + 1, col0 + hh * HEAD_DIM:col0 + (hh + 1) * HEAD_DIM]
            out = jnp.where(sub == first + hh, r, out)
        return out

    for b in range(bb):
        outs, lses = [], []
        for g in range(N_GROUPS):
            c0 = g * 3 * A_WIDTH
            q8 = on_sublanes(b, c0, 0)
            kn8 = on_sublanes(b, c0 + A_WIDTH, 0)
            vn8 = on_sublanes(b, c0 + 2 * A_WIDTH, A_HEADS)
            kv = caches[g][b]
            s = jnp.sum(kv * q8, axis=-1, keepdims=True) * scale
            s = pltpu.roll(jnp.broadcast_to(s, kv.shape), A_HEADS, 1)
            sn = jnp.sum(kn8 * q8, axis=-1, keepdims=True) * scale
            sn = pltpu.roll(jnp.broadcast_to(sn, q8.shape), A_HEADS, 0)
            mx = jnp.maximum(jnp.max(s, axis=0), sn)
            p = jnp.exp(s - mx)
            pn = jnp.exp(sn - mx)
            den = jnp.sum(p, axis=0) + pn
            outs.append((jnp.sum(p * kv, axis=0) + pn * vn8) / den)
            lses.append(mx + jnp.log(den))
        mx = jnp.maximum(jnp.maximum(lses[0], lses[1]), lses[2])
        es = [jnp.exp(l - mx) for l in lses]
        o = (es[0] * outs[0] + es[1] * outs[1] + es[2] * outs[2]) / (es[0] + es[1] + es[2])
        for hh in range(A_HEADS):
            o_ref[b:b + 1, hh * HEAD_DIM:(hh + 1) * HEAD_DIM] = o[A_HEADS + hh:A_HEADS + hh + 1, :]


def _attn_sample(proj, caches, t, b):
    bb = SUBLANES
    in_specs = [pl.BlockSpec((bb, A_COLS), lambda i: (t // bb + i, 0))]
    args = [proj]
    for (win, dil), c in zip(DIL_GROUPS, caches):
        args.append(c.reshape(b, win // dil, dil, 2 * A_HEADS, HEAD_DIM))
        in_specs.append(pl.BlockSpec((bb, win // dil, None, 2 * A_HEADS, HEAD_DIM),
                                     lambda i: (i, 0, 0, 0, 0)))
    return pl.pallas_call(
        functools.partial(_attn_sample_kernel, bb=bb),
        grid=(b // bb,),
        in_specs=in_specs,
        out_specs=pl.BlockSpec((bb, A_WIDTH), lambda i: (i, 0)),
        out_shape=jax.ShapeDtypeStruct((b, A_WIDTH), F32),
        compiler_params=_cparams(("arbitrary",), 40),
        name="attn_sample",
    )(*args)


def _beta_decay_kernel(ba_ref, alog_ref, dtb_ref, ltri_ref, e_ref, beta_ref, gc_ref, *, cumulative):
    x = ba_ref[...]
    lane = lax.broadcasted_iota(jnp.int32, x.shape, 1)
    z = x + dtb_ref[...]
    softplus = jnp.maximum(z, 0.0) + jnp.log1p(jnp.exp(-jnp.abs(z)))
    g = -jnp.exp(alog_ref[...]) * softplus
    if cumulative:
        g = jnp.dot(ltri_ref[...], g, precision=HIGHEST, preferred_element_type=F32)
    vals = jnp.where(lane < B_HEADS, jax.nn.sigmoid(x), g)
    wide = jnp.dot(vals, e_ref[...], precision=HIGHEST, preferred_element_type=F32)
    beta_ref[...] = wide[:, :B_WIDTH]
    gc_ref[...] = wide[:, B_WIDTH:]


def _beta_decay(ba, alog_row, dtb_row, ltri, expand, row0, rows, cumulative):
    blk0 = row0 // CHUNK
    return pl.pallas_call(
        functools.partial(_beta_decay_kernel, cumulative=cumulative),
        grid=(rows // CHUNK,),
        in_specs=[
            pl.BlockSpec((CHUNK, LANES), lambda i: (blk0 + i, 0)),
            pl.BlockSpec((1, LANES), lambda i: (0, 0)),
            pl.BlockSpec((1, LANES), lambda i: (0, 0)),
            pl.BlockSpec((CHUNK, CHUNK), lambda i: (0, 0)),
            pl.BlockSpec((LANES, 2 * B_WIDTH), lambda i: (0, 0)),
        ],
        out_specs=[pl.BlockSpec((CHUNK, B_WIDTH), lambda i: (i, 0)),
                   pl.BlockSpec((CHUNK, B_WIDTH), lambda i: (i, 0))],
        out_shape=[jax.ShapeDtypeStruct((rows, B_WIDTH), F32),
                   jax.ShapeDtypeStruct((rows, B_WIDTH), F32)],
        compiler_params=_cparams(("arbitrary",), 32),
        name="beta_decay",
    )(ba, alog_row, dtb_row, ltri, expand)


def _l2norm_heads(x):
    parts = []
    for hh in range(x.shape[1] // HEAD_DIM):
        a = x[:, hh * HEAD_DIM:(hh + 1) * HEAD_DIM]
        parts.append(a * lax.rsqrt(jnp.sum(a * a, axis=-1, keepdims=True) + EPS))
    return jnp.concatenate(parts, axis=1)


def _gated_norm(o, z, nw):
    y = o * lax.rsqrt(jnp.mean(o * o, axis=-1, keepdims=True) + EPS) * nw
    return y * _silu(z)


INV_BASE = 16


assert CHUNK == LANES


def _unit_lower_inverse(a, eye, row, col):
    c = a[0].shape[0]

    def blk(size):
        sh = size.bit_length() - 1
        return (row >> sh) == (col >> sh)

    base = blk(INV_BASE)
    d = [jnp.where(base, x, 0.0) for x in a]
    tm = [eye - x for x in d]
    pw = [_bf(x) for x in d]
    for _ in range(INV_BASE.bit_length() - 2):
        pw = [_bf(_dot(x, x)) for x in pw]
        tm = [x + _dot(_bf(x), p) for x, p in zip(tm, pw)]
    size = INV_BASE
    while size < c:
        sel = jnp.logical_and(blk(2 * size), jnp.logical_not(blk(size)))
        tm_bf = [_bf(x) for x in tm]
        inner = [_bf(_dot(_bf(jnp.where(sel, x, 0.0)), y)) for x, y in zip(a, tm_bf)]
        tm = [x - _dot(y, z) for x, y, z in zip(tm, tm_bf, inner)]
        size *= 2
    return tm


def _delta_prompt_kernel(x_ref, z_ref, ba_ref, alog_ref, dtb_ref, ltri_ref, cw_ref, nw_ref,
                         o_ref, sfin_ref, s_scr, buf):
    j = pl.program_id(0)
    c = CHUNK
    halo = SUBLANES
    heads = range(B_HEADS)

    @pl.when(j == 0)
    def _():
        s_scr[...] = jnp.zeros_like(s_scr)
        buf[0:halo, :] = jnp.zeros((halo, buf.shape[1]), F32)

    row = lax.broadcasted_iota(jnp.int32, (c, c), 0)
    col = lax.broadcasted_iota(jnp.int32, (c, c), 1)
    tril = col <= row
    strict = col < row
    eye = (col == row).astype(F32)

    logits = ba_ref[...]
    zz = logits + dtb_ref[...]
    softplus = jnp.maximum(zz, 0.0) + jnp.log1p(jnp.exp(-jnp.abs(zz)))
    g_cum = jnp.dot(ltri_ref[...], -jnp.exp(alog_ref[...]) * softplus,
                    precision=HIGHEST, preferred_element_type=F32)
    lane = lax.broadcasted_iota(jnp.int32, logits.shape, 1)
    per_head = jnp.where(lane < B_HEADS, jax.nn.sigmoid(logits), g_cum).T
    gc_t = [jnp.broadcast_to(per_head[B_HEADS + hh:B_HEADS + hh + 1, :], (c, c)) for hh in heads]
    gc = [x.T for x in gc_t]
    beta = [jnp.broadcast_to(per_head[hh:hh + 1, :], (c, c)).T for hh in heads]

    buf[halo:halo + c, :] = x_ref[...]

    def conv(c0):
        cs = slice(c0, c0 + LANES)
        acc = buf[pl.ds(halo, c), cs] * cw_ref[B_CONV - 1:B_CONV, cs]
        for s in range(1, B_CONV):
            acc = acc + buf[pl.ds(halo - s, c), cs] * cw_ref[B_CONV - 1 - s:B_CONV - s, cs]
        return _silu(acc)

    def l2n(a):
        return a * lax.rsqrt(jnp.sum(a * a, axis=-1, keepdims=True) + EPS)

    q = [l2n(conv(hh * LANES)) * (B_DK ** -0.5) for hh in heads]
    k = [l2n(conv(B_WIDTH + hh * LANES)) for hh in heads]
    v = [conv(2 * B_WIDTH + hh * LANES) for hh in heads]
    buf[0:halo, :] = buf[c:c + halo, :]

    k_bf = [_bf(x) for x in k]
    kb = [x * bb for x, bb in zip(k, beta)]
    kq = [_dot_nt(jnp.concatenate([_bf(x), _bf(y)], axis=0), kk) for x, y, kk in zip(kb, q, k_bf)]
    decay = [jnp.exp(jnp.where(tril, g - gt, -jnp.inf)) for g, gt in zip(gc, gc_t)]
    a = [jnp.where(strict, x[:c] * dd, 0.0) for x, dd in zip(kq, decay)]
    qk_bf = [_bf(x[c:] * dd) for x, dd in zip(kq, decay)]
    tm = _unit_lower_inverse(a, eye, row, col)
    e_g = [jnp.exp(g) for g in gc]
    uw = [_dot(_bf(t), jnp.concatenate([_bf(vv * bb), _bf(x * eg)], axis=1))
          for t, vv, bb, x, eg in zip(tm, v, beta, kb, e_g)]
    s = [s_scr[hh] for hh in heads]
    s_bf = [_bf(x) for x in s]
    ws = [_dot(jnp.concatenate([_bf(x[:, LANES:]), _bf(qq * eg)], axis=0), sb)
          for x, qq, eg, sb in zip(uw, q, e_g, s_bf)]
    v_bf = [_bf(x[:, :LANES] - y[:c]) for x, y in zip(uw, ws)]
    o = [y[c:] + _dot(qq, vv) for y, qq, vv in zip(ws, qk_bf, v_bf)]
    g_last = [g[c - 1:c, :] for g in gc]
    kd_bf = [_bf(kk * jnp.exp(gl - g)) for kk, gl, g in zip(k, g_last, gc)]
    for hh in heads:
        sl = slice(hh * LANES, (hh + 1) * LANES)
        s_scr[hh] = s[hh] * jnp.exp(g_last[hh]) + _dot_tn(kd_bf[hh], v_bf[hh])
        o_ref[:, sl] = _gated_norm(o[hh], z_ref[:, sl], nw_ref[...]).astype(BF16)

    @pl.when(j == pl.num_programs(0) - 1)
    def _():
        sfin_ref[...] = s_scr[...]


def _delta_prompt(proj, ba, alog_row, dtb_row, ltri, cw, nw, t, x_col0, z_col0):
    c = CHUNK
    const = lambda shape: pl.BlockSpec(shape, lambda j: (0,) * len(shape))
    return pl.pallas_call(
        _delta_prompt_kernel,
        grid=(t // c,),
        in_specs=[
            pl.BlockSpec((pl.Element(c), pl.Element(B_CONV_COLS)),
                         lambda j: (pl.multiple_of(j * c, c), x_col0)),
            pl.BlockSpec((pl.Element(c), pl.Element(B_WIDTH)),
                         lambda j: (pl.multiple_of(j * c, c), z_col0)),
            pl.BlockSpec((c, LANES), lambda j: (j, 0)),
            const((1, LANES)), const((1, LANES)), const((c, c)), const((B_CONV, B_CONV_COLS)),
            const((1, LANES)),
        ],
        out_specs=[pl.BlockSpec((c, B_WIDTH), lambda j: (j, 0)),
                   const((B_HEADS, B_DK, B_DV))],
        out_shape=[jax.ShapeDtypeStruct((t, B_WIDTH), BF16),
                   jax.ShapeDtypeStruct((B_HEADS, B_DK, B_DV), F32)],
        scratch_shapes=[pltpu.VMEM((B_HEADS, B_DK, B_DV), F32),
                        pltpu.VMEM((c + SUBLANES, B_CONV_COLS), F32)],
        compiler_params=_cparams(("arbitrary",), 40),
        name="delta_prompt",
    )(proj, proj, ba, alog_row, dtb_row, ltri, cw, nw)


def _conv_sample_kernel(x_ref, s0_ref, s1_ref, s2_ref, cw_ref, o_ref):
    c = pl.program_id(0)
    acc = (s0_ref[...] * cw_ref[0:1, :] + s1_ref[...] * cw_ref[1:2, :]
           + s2_ref[...] * cw_ref[2:3, :] + x_ref[...] * cw_ref[3:4, :])
    y = _silu(acc)
    n_qk = 2 * B_HEADS // HEAD_GROUP

    @pl.when(c < n_qk // 2)
    def _():
        o_ref[...] = _l2norm_heads(y) * (B_DK ** -0.5)

    @pl.when(jnp.logical_and(c >= n_qk // 2, c < n_qk))
    def _():
        o_ref[...] = _l2norm_heads(y)

    @pl.when(c >= n_qk)
    def _():
        o_ref[...] = y


def _conv_sample(proj, state, cw, t, b, col0):
    ncb = B_CONV_COLS // COL_BLOCK
    cb0 = col0 // COL_BLOCK
    rb0 = t // b
    sspec = pl.BlockSpec((b, COL_BLOCK), lambda c: (0, c))
    return pl.pallas_call(
        _conv_sample_kernel,
        grid=(ncb,),
        in_specs=[pl.BlockSpec((b, COL_BLOCK), lambda c: (rb0, cb0 + c)), sspec, sspec, sspec,
                  pl.BlockSpec((B_CONV, COL_BLOCK), lambda c: (0, c))],
        out_specs=pl.BlockSpec((b, COL_BLOCK), lambda c: (0, c)),
        out_shape=jax.ShapeDtypeStruct((b, B_CONV_COLS), F32),
        compiler_params=_cparams(("arbitrary",), 32),
        name="conv_sample",
    )(proj, state[:, 0], state[:, 1], state[:, 2], cw)


def _delta_sample_kernel(q_ref, k_ref, v_ref, beta_ref, g_ref, z_ref, nw_ref, s_ref, o_ref, snew_ref, *, bb):
    row = lax.broadcasted_iota(jnp.int32, (LANES, LANES), 0)
    col = lax.broadcasted_iota(jnp.int32, (LANES, LANES), 1)
    eye = col == row

    def column(r):
        return jnp.sum(jnp.where(eye, r, 0.0), axis=-1, keepdims=True)

    units = [(b, hh, slice(hh * LANES, (hh + 1) * LANES)) for b in range(bb) for hh in range(HEAD_GROUP)]
    rows = [(q_ref[b:b + 1, sl], k_ref[b:b + 1, sl]) for b, _, sl in units]
    cols = [(column(q), column(k)) for q, k in rows]
    qk = [jnp.sum(q * k, axis=-1, keepdims=True) for q, k in rows]
    reads = [(jnp.sum(s_ref[b, hh] * qc, axis=0, keepdims=True),
              jnp.sum(s_ref[b, hh] * kc, axis=0, keepdims=True))
             for (b, hh, _), (qc, kc) in zip(units, cols)]
    outs = []
    for (b, hh, sl), (_, k_col), (qs, ks), qk_u in zip(units, cols, reads, qk):
        e_g = jnp.exp(g_ref[b:b + 1, sl])
        v_new = beta_ref[b:b + 1, sl] * (v_ref[b:b + 1, sl] - e_g * ks)
        outs.append(e_g * qs + qk_u * v_new)
        snew_ref[b, hh] = s_ref[b, hh] * e_g + k_col * v_new
    for (b, _, sl), o in zip(units, outs):
        o_ref[b:b + 1, sl] = _gated_norm(o, z_ref[b:b + 1, sl], nw_ref[...])


def _delta_sample(qkv, beta, g, proj, nw, state, t, b, z_col0):
    bb = SUBLANES
    hgs = B_HEADS // HEAD_GROUP
    zb0 = z_col0 // COL_BLOCK
    rb0 = t // bb
    blk = lambda off: pl.BlockSpec((bb, COL_BLOCK), lambda i, h, off=off: (i, off + h))
    sspec = pl.BlockSpec((bb, HEAD_GROUP, B_DK, B_DV), lambda i, h: (i, h, 0, 0))
    return pl.pallas_call(
        functools.partial(_delta_sample_kernel, bb=bb),
        grid=(b // bb, hgs),
        in_specs=[blk(0), blk(hgs), blk(2 * hgs), blk(0), blk(0),
                  pl.BlockSpec((bb, COL_BLOCK), lambda i, h: (rb0 + i, zb0 + h)),
                  pl.BlockSpec((1, LANES), lambda i, h: (0, 0)), sspec],
        out_specs=[pl.BlockSpec((bb, COL_BLOCK), lambda i, h: (i, h)), sspec],
        out_shape=[jax.ShapeDtypeStruct((b, B_WIDTH), F32),
                   jax.ShapeDtypeStruct(state.shape, F32)],
        compiler_params=_cparams(("arbitrary", "arbitrary"), 32),
        name="delta_sample",
    )(qkv, qkv, qkv, beta, g, proj, nw, state)


def _ffn_kernel(h_ref, hs_ref, wg_ref, wu_ref, cw_ref, s0_ref, s1_ref, act_ref, tail_ref, gs_ref,
                wg_bf, wu_bf, buf, *, tm, bs):
    i = pl.program_id(1)
    halo = SUBLANES

    @pl.when(i == 0)
    def _():
        wg_bf[...] = _bf(wg_ref[...])
        wu_bf[...] = _bf(wu_ref[...])
        buf[...] = jnp.zeros(buf.shape, F32)
        h = hs_ref[...]
        g = _dot(h, wg_bf[...])
        gs_ref[...] = g
        acc = s0_ref[...] * cw_ref[0:1, :] + s1_ref[...] * cw_ref[1:2, :] + g * cw_ref[2:3, :]
        act_ref[0:bs, :] = _bf(_silu(acc) * _dot(h, wu_bf[...]))

    @pl.when(i > 0)
    def _():
        rows = tm // ROW_PARTS
        parts = [slice(p * rows, (p + 1) * rows) for p in range(ROW_PARTS)]
        gu = [(_dot(h_ref[sl, :], wg_bf[...]), _dot(h_ref[sl, :], wu_bf[...])) for sl in parts]
        prev = buf[...]
        first_rows = lax.broadcasted_iota(jnp.int32, prev.shape, 0)
        for sl, (g, u) in zip(parts, gu):
            acc = g * cw_ref[FFN_CONV - 1:FFN_CONV, :]
            for s in range(1, FFN_CONV):
                rolled = pltpu.roll(g, s, 0)
                head = jnp.where(first_rows < s, pltpu.roll(prev, s, 0), rolled[0:halo])
                shifted = jnp.concatenate([head, rolled[halo:]], axis=0)
                acc = acc + shifted * cw_ref[FFN_CONV - 1 - s:FFN_CONV - s, :]
            prev = g[rows - halo:rows]
            act_ref[sl, :] = _bf(_silu(acc) * u)
        buf[...] = prev
        tail_ref[...] = prev


def _ffn(h2, wg, wu, cw, state, t, b):
    m, d = h2.shape
    f = wg.shape[1]
    tf = 256
    tm = _row_tile(t, 1024)
    npt = t // tm
    assert b <= tm and m == t + b and t % b == 0
    sspec = pl.BlockSpec((b, tf), lambda c, i: (0, c))
    nf = f // tf

    def weight_spec(switch_step):
        return pl.BlockSpec(
            (d, tf), lambda c, i: (0, jnp.minimum(c + (i >= switch_step).astype(jnp.int32), nf - 1)))

    return pl.pallas_call(
        functools.partial(_ffn_kernel, tm=tm, bs=b),
        grid=(nf, npt + 1),
        in_specs=[
            pl.BlockSpec((tm, d), lambda c, i: (jnp.maximum(i - 1, 0), 0)),
            pl.BlockSpec((b, d), lambda c, i: (t // b, 0)),
            weight_spec(2),
            weight_spec(min(4, npt)),
            pl.BlockSpec((FFN_CONV, tf), lambda c, i: (0, c)),
            sspec, sspec,
        ],
        out_specs=[pl.BlockSpec((tm, tf), lambda c, i: (jnp.where(i == 0, npt, i - 1), c)),
                   pl.BlockSpec((SUBLANES, tf), lambda c, i: (0, c)),
                   pl.BlockSpec((b, tf), lambda c, i: (0, c))],
        out_shape=[jax.ShapeDtypeStruct((m, f), BF16),
                   jax.ShapeDtypeStruct((SUBLANES, f), F32),
                   jax.ShapeDtypeStruct((b, f), F32)],
        scratch_shapes=[pltpu.VMEM((d, tf), BF16), pltpu.VMEM((d, tf), BF16),
                        pltpu.VMEM((SUBLANES, tf), F32)],
        compiler_params=_cparams(("arbitrary", "arbitrary"), 56),
        name="ffn",
    )(h2, h2, wg, wu, cw, state[:, 0], state[:, 1])


def _rope_tables(t, b):
    inv = ROPE_THETA ** (-jnp.arange(0, ROT_DIM, 2, dtype=F32) / ROT_DIM)
    pos = jnp.concatenate([jnp.arange(t, dtype=jnp.int32),
                           jnp.full((b,), PAST_LEN, jnp.int32)]).astype(F32)
    ang = pos[:, None] * inv[None, :]
    cos, sin = jnp.cos(ang), jnp.sin(ang)
    m = t + b
    pad = LANES - ROT_DIM
    cos_t = jnp.concatenate([cos, cos, jnp.ones((m, pad), F32)], axis=1)
    sin_t = jnp.concatenate([-sin, sin, jnp.zeros((m, pad), F32)], axis=1)
    return cos_t, sin_t


def kernel(x_prompt, x_sample, cache_kv_w128, cache_kv_w512, cache_kv_w2048, state_conv_qkv, state_delta,
           state_ffn_conv, norm_mix, w_in, conv_qkv, a_log, dt_bias, delta_norm, w_a_out, w_b_out, w_out,
           norm_ffn, w_gate, ffn_conv, w_up, w_down, norm_final):
    assert norm_mix.shape[0] == 1, "single layer only"
    assert x_prompt.shape[0] == 1 and x_sample.shape[1] == 1
    t, d = x_prompt.shape[1], x_prompt.shape[2]
    b = x_sample.shape[0]
    assert t % SUPER == 0 and b == ROW_BLOCK
    caches = (cache_kv_w128[0], cache_kv_w512[0], cache_kv_w2048[0])
    for (win, _), c in zip(DIL_GROUPS, caches):
        assert c.shape[1] == win, "cache must hold a full window"

    off_bqkv = A_COLS
    off_bz = off_bqkv + B_CONV_COLS
    off_ba = off_bz + B_WIDTH
    off_gate = off_ba + 2 * B_HEADS
    assert w_in.shape[2] == off_gate + 2 * d
    w_in_t = jnp.swapaxes(w_in[0], 0, 1)
    w_main = _bf(w_in_t)
    w_ba = jnp.pad(w_in_t[off_ba:off_gate], ((0, LANES - 2 * B_HEADS), (0, 0)))
    w_ba_hi = _bf(w_ba)
    w_ba = jnp.concatenate([w_ba_hi, _bf(w_ba - w_ba_hi.astype(F32))], axis=0)
    gate_col0 = off_ba

    xp = x_prompt.reshape(t, d)
    xs = x_sample.reshape(b, d)
    h, x_cat, ba = _norm_in(xp, xs, norm_mix, w_ba)
    cos_t, sin_t = _rope_tables(t, b)
    proj = _proj(h, w_main, cos_t, sin_t, gate_col0, off_gate)

    oa_p = _attn_prompt(proj, t)
    oa_s = _attn_sample(proj, caches, t, b)
    o_a = jnp.concatenate([oa_p, _bf(oa_s)], axis=0)

    alog_row = jnp.pad(a_log[0], (B_HEADS, LANES - 2 * B_HEADS)).reshape(1, LANES)
    dtb_row = jnp.pad(dt_bias[0], (B_HEADS, LANES - 2 * B_HEADS)).reshape(1, LANES)
    ltri = jnp.tril(jnp.ones((CHUNK, CHUNK), F32))
    head_of_col = jnp.arange(2 * B_WIDTH, dtype=jnp.int32) // LANES
    expand = (jnp.arange(LANES, dtype=jnp.int32)[:, None] == head_of_col[None, :]).astype(F32)
    cw_qkv = conv_qkv[0]
    ob_p, delta_p = _delta_prompt(proj, ba, alog_row, dtb_row, ltri, cw_qkv, delta_norm, t, off_bqkv, off_bz)
    beta_s, g_s = _beta_decay(ba, alog_row, dtb_row, ltri, expand, t, b, False)
    qkv_s = _conv_sample(proj, state_conv_qkv[0], cw_qkv, t, b, off_bqkv)
    ob_s, delta_s = _delta_sample(qkv_s, beta_s, g_s, proj, delta_norm, state_delta[0], t, b, off_bz)
    o_b = jnp.concatenate([ob_p, _bf(ob_s)], axis=0)

    merged = _merge(o_a, o_b, _bf(w_a_out[0]), _bf(w_b_out[0]), proj, gate_col0)
    x1 = _mm_res(merged, _bf(w_out[0]), x_cat, ROW_TILE_CAP, 512, "out_proj")

    h2 = _norm_mid(x1, norm_ffn)
    act, g_tail, g_smp = _ffn(h2, w_gate[0], w_up[0], ffn_conv[0], state_ffn_conv[0], t, b)
    x2 = _mm_res(act, _bf(w_down[0]), x1, 640, 256, "down_proj")
    y_p, y_s = _norm_out(x2, norm_final.reshape(1, d), t, b)

    kv_p, kv_s = [], []
    for g, (win, _) in enumerate(DIL_GROUPS):
        c0 = g * 3 * A_WIDTH + A_WIDTH
        keep = min(win, t)
        kv_p.append(proj[t - keep:t, c0:c0 + 2 * A_WIDTH].reshape(1, 1, keep, 2, A_HEADS, HEAD_DIM))
        kv_s.append(proj[t:, c0:c0 + 2 * A_WIDTH].reshape(1, b, 1, 2, A_HEADS, HEAD_DIM))
    conv_p = proj[t - (B_CONV - 1):t, off_bqkv:off_bz].reshape(1, 1, B_CONV - 1, B_CONV_COLS)
    conv_s = jnp.concatenate([state_conv_qkv[0][:, 1:], proj[t:, off_bqkv:off_bz][:, None]], axis=1)[None]
    ffn_p = g_tail[SUBLANES - (FFN_CONV - 1):].reshape(1, 1, FFN_CONV - 1, -1)
    ffn_s = jnp.concatenate([state_ffn_conv[0][:, 1:], g_smp[:, None]], axis=1)[None]
    return (y_p.reshape(1, t, d), y_s.reshape(b, 1, d), kv_p[0], kv_p[1], kv_p[2],
            conv_p, delta_p.reshape(1, 1, B_HEADS, B_DK, B_DV), ffn_p,
            kv_s[0], kv_s[1], kv_s[2], conv_s, delta_s[None], ffn_s)
```

```python
import functools

import jax
import jax.numpy as jnp
from jax import lax
from jax.experimental import pallas as pl
from jax.experimental.pallas import tpu as pltpu

F32 = jnp.float32
BF16 = jnp.bfloat16
HIGHEST = lax.Precision.HIGHEST

HEAD_DIM = 128
ROT_DIM = HEAD_DIM // 4
ROT_HALF = ROT_DIM // 2
ROPE_THETA = 500000.0
DIL_GROUPS = ((128, 1), (512, 4), (2048, 16))
N_GROUPS = len(DIL_GROUPS)
A_HEADS = 4
A_WIDTH = A_HEADS * HEAD_DIM
A_COLS = N_GROUPS * 3 * A_WIDTH
KEYS_PER_QUERY = 128
B_HEADS = 16
B_DK = 128
B_DV = 128
B_WIDTH = B_HEADS * B_DK
B_CONV = 4
B_CONV_COLS = 3 * B_WIDTH
FFN_CONV = 3
EPS = 1e-6
PAST_LEN = 2048

LANES = 128
SUBLANES = 8
ROW_BLOCK = 128
SUPER = 2048
CHUNK = 128
HEAD_GROUP = 4
COL_BLOCK = HEAD_GROUP * LANES
VMEM_CAP_MB = 60
ROW_PARTS = 4
ROW_TILE_CAP = 1664


def _cparams(sem, vmem_mb):
    return pltpu.CompilerParams(dimension_semantics=sem,
                                vmem_limit_bytes=int(min(vmem_mb, VMEM_CAP_MB) * 2 ** 20))


def _bf(x):
    return x.astype(BF16)


def _dot(a, b):
    return jnp.dot(a, b, preferred_element_type=F32)


def _dot_nt(a, b):
    return lax.dot_general(a, b, (((1,), (1,)), ((), ())), preferred_element_type=F32)


def _dot_tn(a, b):
    return lax.dot_general(a, b, (((0,), (0,)), ((), ())), preferred_element_type=F32)


def _silu(x):
    return x * jax.nn.sigmoid(x)


NORM_ROWS = 256


def _norm_in_kernel(xp_ref, xs_ref, nw_ref, wba_ref, h_ref, xcat_ref, ba_ref, *, n_prompt_blocks, bs):
    i = pl.program_id(0)

    def body(x, rows):
        y = x * lax.rsqrt(jnp.mean(x * x, axis=-1, keepdims=True) + EPS) * nw_ref[...]
        y_hi = _bf(y)
        y_lo = _bf(y - y_hi.astype(F32))
        h_ref[rows, :] = y_hi
        xcat_ref[rows, :] = x
        hi = _dot_nt(y_hi, wba_ref[...])
        ba_ref[rows, :] = hi[:, :LANES] + hi[:, LANES:] + _dot_nt(y_lo, wba_ref[0:LANES, :])

    @pl.when(i < n_prompt_blocks)
    def _():
        body(xp_ref[...], slice(None))

    @pl.when(i == n_prompt_blocks)
    def _():
        body(xs_ref[...], slice(0, bs))


def _norm_in(xp, xs, nw, wba):
    t, d = xp.shape
    b = xs.shape[0]
    rb = NORM_ROWS
    npb = t // rb
    m = t + b
    return pl.pallas_call(
        functools.partial(_norm_in_kernel, n_prompt_blocks=npb, bs=b),
        grid=(npb + 1,),
        in_specs=[
            pl.BlockSpec((rb, d), lambda i: (jnp.minimum(i, npb - 1), 0)),
            pl.BlockSpec((b, d), lambda i: (0, 0)),
            pl.BlockSpec((1, d), lambda i: (0, 0)),
            pl.BlockSpec((2 * LANES, d), lambda i: (0, 0)),
        ],
        out_specs=[
            pl.BlockSpec((rb, d), lambda i: (i, 0)),
            pl.BlockSpec((rb, d), lambda i: (i, 0)),
            pl.BlockSpec((rb, LANES), lambda i: (i, 0)),
        ],
        out_shape=[
            jax.ShapeDtypeStruct((m, d), BF16),
            jax.ShapeDtypeStruct((m, d), F32),
            jax.ShapeDtypeStruct((m, LANES), F32),
        ],
        compiler_params=_cparams(("arbitrary",), 40),
        name="norm_in",
    )(xp, xs, nw, wba)


def _norm_mid_kernel(x_ref, nw_ref, h_ref):
    x = x_ref[...]
    y = x * lax.rsqrt(jnp.mean(x * x, axis=-1, keepdims=True) + EPS) * nw_ref[...]
    h_ref[...] = _bf(y)


def _norm_mid(x, nw):
    m, d = x.shape
    rb = _row_tile(m, 640)
    return pl.pallas_call(
        _norm_mid_kernel,
        grid=(m // rb,),
        in_specs=[pl.BlockSpec((rb, d), lambda i: (i, 0)),
                  pl.BlockSpec((1, d), lambda i: (0, 0))],
        out_specs=pl.BlockSpec((rb, d), lambda i: (i, 0)),
        out_shape=jax.ShapeDtypeStruct((m, d), BF16),
        compiler_params=_cparams(("arbitrary",), 40),
        name="norm_mid",
    )(x, nw)


def _norm_out_kernel(x_ref, nw_ref, yp_ref, ys_ref, *, n_prompt_blocks, bs):
    i = pl.program_id(0)

    def norm(x):
        return x * lax.rsqrt(jnp.mean(x * x, axis=-1, keepdims=True) + EPS) * nw_ref[...]

    @pl.when(i < n_prompt_blocks)
    def _():
        yp_ref[...] = norm(x_ref[...])

    @pl.when(i == n_prompt_blocks)
    def _():
        ys_ref[...] = norm(x_ref[0:bs, :])


def _norm_out(x, nw, t, b):
    d = x.shape[1]
    rb = NORM_ROWS
    npb = t // rb
    return pl.pallas_call(
        functools.partial(_norm_out_kernel, n_prompt_blocks=npb, bs=b),
        grid=(npb + 1,),
        in_specs=[pl.BlockSpec((rb, d), lambda i: (i, 0)),
                  pl.BlockSpec((1, d), lambda i: (0, 0))],
        out_specs=[
            pl.BlockSpec((rb, d), lambda i: (jnp.minimum(i, npb - 1), 0)),
            pl.BlockSpec((b, d), lambda i: (0, 0)),
        ],
        out_shape=[jax.ShapeDtypeStruct((t, d), F32), jax.ShapeDtypeStruct((b, d), F32)],
        compiler_params=_cparams(("arbitrary",), 32),
        name="norm_out",
    )(x, nw)


def _row_tile(m, cap):
    best = 16
    for c in range(16, cap + 1, 16):
        if m % c == 0:
            best = c
    return best


def _proj_kernel(h_ref, w_ref, cos_ref, sin_ref, o_ref, *, n_attn_tiles, gate_tile0):
    j = pl.program_id(1)
    is_rope = jnp.logical_and(j < n_attn_tiles, j % 3 != 2)
    is_gate = j >= gate_tile0

    rows = h_ref.shape[0] // ROW_PARTS
    parts = [slice(p * rows, (p + 1) * rows) for p in range(ROW_PARTS)]

    def products():
        return [_dot_nt(h_ref[sl, :], w_ref[...]) for sl in parts]

    @pl.when(is_rope)
    def _():
        for sl, acc in zip(parts, products()):
            cos = cos_ref[sl, :]
            sin = sin_ref[sl, :]
            lane = lax.broadcasted_iota(jnp.int32, cos.shape, 1)
            for hh in range(A_HEADS):
                a = acc[:, hh * HEAD_DIM:(hh + 1) * HEAD_DIM]
                partner = jnp.where(lane < ROT_HALF,
                                    pltpu.roll(a, HEAD_DIM - ROT_HALF, 1),
                                    pltpu.roll(a, ROT_HALF, 1))
                o_ref[sl, hh * HEAD_DIM:(hh + 1) * HEAD_DIM] = a * cos + partner * sin

    @pl.when(is_gate)
    def _():
        for sl, acc in zip(parts, products()):
            o_ref[sl, :] = jax.nn.sigmoid(acc)

    @pl.when(jnp.logical_not(jnp.logical_or(is_rope, is_gate)))
    def _():
        for sl, acc in zip(parts, products()):
            o_ref[sl, :] = acc


def _proj(h, w, cos, sin, gate_col0, gate_row0):
    m, k = h.shape
    skip = gate_row0 - gate_col0
    n = w.shape[0] - skip
    tn = A_WIDTH
    tm = _row_tile(m, ROW_TILE_CAP)
    gate_tile0 = gate_col0 // tn
    return pl.pallas_call(
        functools.partial(_proj_kernel, n_attn_tiles=A_COLS // tn, gate_tile0=gate_tile0),
        grid=(m // tm, n // tn),
        in_specs=[
            pl.BlockSpec((tm, k), lambda i, j: (i, 0)),
            pl.BlockSpec((pl.Element(tn), pl.Element(k)),
                         lambda i, j: (pl.multiple_of(j * tn + jnp.where(j >= gate_tile0, skip, 0),
                                                      2 * SUBLANES), 0)),
            pl.BlockSpec((tm, LANES), lambda i, j: (i, 0)),
            pl.BlockSpec((tm, LANES), lambda i, j: (i, 0)),
        ],
        out_specs=pl.BlockSpec((tm, tn), lambda i, j: (i, j)),
        out_shape=jax.ShapeDtypeStruct((m, n), F32),
        compiler_params=_cparams(("arbitrary", "arbitrary"), 54),
        name="proj_in",
    )(h, w, cos, sin)


def _merge_kernel(oa_ref, ob_ref, wa_ref, wb_ref, ga_ref, gb_ref, o_ref):
    a = _dot(oa_ref[...], wa_ref[...])
    b = _dot(ob_ref[...], wb_ref[...])
    o_ref[...] = _bf(ga_ref[...] * a + gb_ref[...] * b)


def _merge(oa, ob, wa, wb, proj, gate_col0):
    m = oa.shape[0]
    n = wa.shape[1]
    tn = COL_BLOCK
    tm = _row_tile(m, ROW_TILE_CAP)
    ga0 = gate_col0 // tn
    gb0 = (gate_col0 + n) // tn
    return pl.pallas_call(
        _merge_kernel,
        grid=(m // tm, n // tn),
        in_specs=[
            pl.BlockSpec((tm, oa.shape[1]), lambda i, j: (i, 0)),
            pl.BlockSpec((tm, ob.shape[1]), lambda i, j: (i, 0)),
            pl.BlockSpec((wa.shape[0], tn), lambda i, j: (0, j)),
            pl.BlockSpec((wb.shape[0], tn), lambda i, j: (0, j)),
            pl.BlockSpec((tm, tn), lambda i, j: (i, ga0 + j)),
            pl.BlockSpec((tm, tn), lambda i, j: (i, gb0 + j)),
        ],
        out_specs=pl.BlockSpec((tm, tn), lambda i, j: (i, j)),
        out_shape=jax.ShapeDtypeStruct((m, n), BF16),
        compiler_params=_cparams(("arbitrary", "arbitrary"), 48),
        name="merge",
    )(oa, ob, wa, wb, proj, proj)


def _mm_res_kernel(a_ref, w_ref, r_ref, o_ref):
    o_ref[...] = r_ref[...] + _dot(a_ref[...], w_ref[...])


def _mm_res(a, w, res, tm_cap, tn, name):
    m, k = a.shape
    n = w.shape[1]
    tm = _row_tile(m, tm_cap)
    return pl.pallas_call(
        _mm_res_kernel,
        grid=(m // tm, n // tn),
        in_specs=[
            pl.BlockSpec((tm, k), lambda i, j: (i, 0)),
            pl.BlockSpec((k, tn), lambda i, j: (0, j)),
            pl.BlockSpec((tm, tn), lambda i, j: (i, j)),
        ],
        out_specs=pl.BlockSpec((tm, tn), lambda i, j: (i, j)),
        out_shape=jax.ShapeDtypeStruct((m, n), F32),
        compiler_params=_cparams(("arbitrary", "arbitrary"), 60),
        name=name,
    )(a, w, res)


ATTN_TILE_BATCH = 8


def _attn_prompt_kernel(*refs):
    grp = [refs[5 * g:5 * g + 5] for g in range(N_GROUPS)]
    o_ref, o_scr, l_scr = refs[5 * N_GROUPS:]
    first = pl.program_id(1) == 0
    row = lax.broadcasted_iota(jnp.int32, (LANES, LANES), 0)
    col = lax.broadcasted_iota(jnp.int32, (LANES, LANES), 1)
    mask_cur = col <= row
    mask_prev = col >= row
    mask_prev_first = col >= row + jnp.where(first, LANES, 0)
    scale = HEAD_DIM ** -0.5
    nq = KEYS_PER_QUERY

    def rows(dil, base, r):
        if dil == 1:
            return pl.ds(base, nq)
        return pl.ds(base + r, nq, stride=dil)

    tiles = [(g, dil, sb, r) for g, (_, dil) in enumerate(DIL_GROUPS)
             for sb in range(SUPER // (nq * dil)) for r in range(dil)]
    for t0 in range(0, len(tiles), ATTN_TILE_BATCH):
        batch = tiles[t0:t0 + ATTN_TILE_BATCH]
        scores = []
        for g, dil, sb, r in batch:
            q_ref, k_ref, v_ref, kp_ref, vp_ref = grp[g]
            span = nq * dil
            cur = rows(dil, sb * span, r)
            qt = _bf(q_ref[cur, :])
            if sb > 0:
                prev = rows(dil, (sb - 1) * span, r)
                kp, vp, mp = k_ref[prev, :], v_ref[prev, :], mask_prev
            else:
                prev = rows(dil, 0, r)
                kp, vp, mp = kp_ref[prev, :], vp_ref[prev, :], mask_prev_first
            s_c = jnp.where(mask_cur, _dot_nt(qt, _bf(k_ref[cur, :])) * scale, -jnp.inf)
            s_p = jnp.where(mp, _dot_nt(qt, _bf(kp)) * scale, -jnp.inf)
            scores.append((s_c, s_p, _bf(v_ref[cur, :]), _bf(vp)))
        probs = []
        for s_c, s_p, vc, vp in scores:
            mx = jnp.maximum(jnp.max(s_c, axis=-1, keepdims=True),
                             jnp.max(s_p, axis=-1, keepdims=True))
            p_c = jnp.exp(s_c - mx)
            p_p = jnp.exp(s_p - mx)
            den = jnp.sum(p_c, axis=-1, keepdims=True) + jnp.sum(p_p, axis=-1, keepdims=True)
            probs.append((_bf(p_c), _bf(p_p), vc, vp, den, mx))
        for (g, dil, sb, r), (p_c, p_p, vc, vp, den, mx) in zip(batch, probs):
            o = (_dot(p_c, vc) + _dot(p_p, vp)) / den
            dst = rows(dil, sb * nq * dil, r)
            o_scr[g, dst, :] = o
            l_scr[g, dst, :] = jnp.broadcast_to(mx + jnp.log(den), (nq, LANES))

    l0, l1, l2 = l_scr[0], l_scr[1], l_scr[2]
    mx = jnp.maximum(jnp.maximum(l0, l1), l2)
    e0, e1, e2 = jnp.exp(l0 - mx), jnp.exp(l1 - mx), jnp.exp(l2 - mx)
    o = (e0 * o_scr[0] + e1 * o_scr[1] + e2 * o_scr[2]) / (e0 + e1 + e2)
    o_ref[...] = _bf(o)


def _attn_prompt(proj, t):
    nsup = t // SUPER
    in_specs = []
    args = []
    for g, (_, dil) in enumerate(DIL_GROUPS):
        span = KEYS_PER_QUERY * dil
        per = SUPER // span
        cb = g * 3 * A_HEADS
        for part in range(3):
            in_specs.append(pl.BlockSpec(
                (SUPER, HEAD_DIM), lambda h, i, c=cb + part * A_HEADS: (i, c + h)))
            args.append(proj)
        for part in (1, 2):
            in_specs.append(pl.BlockSpec(
                (span, HEAD_DIM),
                lambda h, i, c=cb + part * A_HEADS, per=per: (jnp.maximum(i * per - 1, 0), c + h)))
            args.append(proj)
    return pl.pallas_call(
        _attn_prompt_kernel,
        grid=(A_HEADS, nsup),
        in_specs=in_specs,
        out_specs=pl.BlockSpec((SUPER, HEAD_DIM), lambda h, i: (i, h)),
        out_shape=jax.ShapeDtypeStruct((t, A_WIDTH), BF16),
        scratch_shapes=[pltpu.VMEM((N_GROUPS, SUPER, HEAD_DIM), F32),
                        pltpu.VMEM((N_GROUPS, SUPER, HEAD_DIM), F32)],
        compiler_params=_cparams(("arbitrary", "arbitrary"), 48),
        name="attn_prompt",
    )(*args)


def _attn_sample_kernel(qkv_ref, c0_ref, c1_ref, c2_ref, o_ref, *, bb):
    caches = (c0_ref, c1_ref, c2_ref)
    scale = HEAD_DIM ** -0.5
    sub = lax.broadcasted_iota(jnp.int32, (2 * A_HEADS, HEAD_DIM), 0)

    def on_sublanes(b, col0, first):
        out = jnp.zeros((2 * A_HEADS, HEAD_DIM), F32)
        for hh in range(A_HEADS):
            r = qkv_ref[b:b + 1, col0 + hh * HEAD_DIM:col0 + (hh + 1) * HEAD_DIM]
            out = jnp.where(sub == first + hh, r, out)
        return out

    for b in range(bb):
        outs, lses = [], []
        for g in range(N_GROUPS):
            c0 = g * 3 * A_WIDTH
            q8 = on_sublanes(b, c0, 0)
            kn8 = on_sublanes(b, c0 + A_WIDTH, 0)
            vn8 = on_sublanes(b, c0 + 2 * A_WIDTH, A_HEADS)
            kv = caches[g][b]
            s = jnp.sum(kv * q8, axis=-1, keepdims=True) * scale
            s = pltpu.roll(jnp.broadcast_to(s, kv.shape), A_HEADS, 1)
            sn = jnp.sum(kn8 * q8, axis=-1, keepdims=True) * scale
            sn = pltpu.roll(jnp.broadcast_to(sn, q8.shape), A_HEADS, 0)
            mx = jnp.maximum(jnp.max(s, axis=0), sn)
            p = jnp.exp(s - mx)
            pn = jnp.exp(sn - mx)
            den = jnp.sum(p, axis=0) + pn
            outs.append((jnp.sum(p * kv, axis=0) + pn * vn8) / den)
            lses.append(mx + jnp.log(den))
        mx = jnp.maximum(jnp.maximum(lses[0], lses[1]), lses[2])
        es = [jnp.exp(l - mx) for l in lses]
        o = (es[0] * outs[0] + es[1] * outs[1] + es[2] * outs[2]) / (es[0] + es[1] + es[2])
        for hh in range(A_HEADS):
            o_ref[b:b + 1, hh * HEAD_DIM:(hh + 1) * HEAD_DIM] = o[A_HEADS + hh:A_HEADS + hh + 1, :]


def _attn_sample(proj, caches, t, b):
    bb = SUBLANES
    in_specs = [pl.BlockSpec((bb, A_COLS), lambda i: (t // bb + i, 0))]
    args = [proj]
    for (win, dil), c in zip(DIL_GROUPS, caches):
        args.append(c.reshape(b, win // dil, dil, 2 * A_HEADS, HEAD_DIM))
        in_specs.append(pl.BlockSpec((bb, win // dil, None, 2 * A_HEADS, HEAD_DIM),
                                     lambda i: (i, 0, 0, 0, 0)))
    return pl.pallas_call(
        functools.partial(_attn_sample_kernel, bb=bb),
        grid=(b // bb,),
        in_specs=in_specs,
        out_specs=pl.BlockSpec((bb, A_WIDTH), lambda i: (i, 0)),
        out_shape=jax.ShapeDtypeStruct((b, A_WIDTH), F32),
        compiler_params=_cparams(("arbitrary",), 40),
        name="attn_sample",
    )(*args)


def _beta_decay_kernel(ba_ref, alog_ref, dtb_ref, ltri_ref, e_ref, beta_ref, gc_ref, *, cumulative):
    x = ba_ref[...]
    lane = lax.broadcasted_iota(jnp.int32, x.shape, 1)
    z = x + dtb_ref[...]
    softplus = jnp.maximum(z, 0.0) + jnp.log1p(jnp.exp(-jnp.abs(z)))
    g = -jnp.exp(alog_ref[...]) * softplus
    if cumulative:
        g = jnp.dot(ltri_ref[...], g, precision=HIGHEST, preferred_element_type=F32)
    vals = jnp.where(lane < B_HEADS, jax.nn.sigmoid(x), g)
    wide = jnp.dot(vals, e_ref[...], precision=HIGHEST, preferred_element_type=F32)
    beta_ref[...] = wide[:, :B_WIDTH]
    gc_ref[...] = wide[:, B_WIDTH:]


def _beta_decay(ba, alog_row, dtb_row, ltri, expand, row0, rows, cumulative):
    blk0 = row0 // CHUNK
    return pl.pallas_call(
        functools.partial(_beta_decay_kernel, cumulative=cumulative),
        grid=(rows // CHUNK,),
        in_specs=[
            pl.BlockSpec((CHUNK, LANES), lambda i: (blk0 + i, 0)),
            pl.BlockSpec((1, LANES), lambda i: (0, 0)),
            pl.BlockSpec((1, LANES), lambda i: (0, 0)),
            pl.BlockSpec((CHUNK, CHUNK), lambda i: (0, 0)),
            pl.BlockSpec((LANES, 2 * B_WIDTH), lambda i: (0, 0)),
        ],
        out_specs=[pl.BlockSpec((CHUNK, B_WIDTH), lambda i: (i, 0)),
                   pl.BlockSpec((CHUNK, B_WIDTH), lambda i: (i, 0))],
        out_shape=[jax.ShapeDtypeStruct((rows, B_WIDTH), F32),
                   jax.ShapeDtypeStruct((rows, B_WIDTH), F32)],
        compiler_params=_cparams(("arbitrary",), 32),
        name="beta_decay",
    )(ba, alog_row, dtb_row, ltri, expand)


def _l2norm_heads(x):
    parts = []
    for hh in range(x.shape[1] // HEAD_DIM):
        a = x[:, hh * HEAD_DIM:(hh + 1) * HEAD_DIM]
        parts.append(a * lax.rsqrt(jnp.sum(a * a, axis=-1, keepdims=True) + EPS))
    return jnp.concatenate(parts, axis=1)


def _gated_norm(o, z, nw):
    y = o * lax.rsqrt(jnp.mean(o * o, axis=-1, keepdims=True) + EPS) * nw
    return y * _silu(z)


INV_BASE = 16


assert CHUNK == LANES


def _unit_lower_inverse(a, eye, row, col):
    c = a[0].shape[0]

    def blk(size):
        sh = size.bit_length() - 1
        return (row >> sh) == (col >> sh)

    base = blk(INV_BASE)
    d = [jnp.where(base, x, 0.0) for x in a]
    tm = [eye - x for x in d]
    pw = [_bf(x) for x in d]
    for _ in range(INV_BASE.bit_length() - 2):
        pw = [_bf(_dot(x, x)) for x in pw]
        tm = [x + _dot(_bf(x), p) for x, p in zip(tm, pw)]
    size = INV_BASE
    while size < c:
        sel = jnp.logical_and(blk(2 * size), jnp.logical_not(blk(size)))
        tm_bf = [_bf(x) for x in tm]
        inner = [_bf(_dot(_bf(jnp.where(sel, x, 0.0)), y)) for x, y in zip(a, tm_bf)]
        tm = [x - _dot(y, z) for x, y, z in zip(tm, tm_bf, inner)]
        size *= 2
    return tm


def _delta_prompt_kernel(x_ref, z_ref, ba_ref, alog_ref, dtb_ref, ltri_ref, cw_ref, nw_ref,
                         o_ref, sfin_ref, s_scr, buf):
    j = pl.program_id(0)
    c = CHUNK
    halo = SUBLANES
    heads = range(B_HEADS)

    @pl.when(j == 0)
    def _():
        s_scr[...] = jnp.zeros_like(s_scr)
        buf[0:halo, :] = jnp.zeros((halo, buf.shape[1]), F32)

    row = lax.broadcasted_iota(jnp.int32, (c, c), 0)
    col = lax.broadcasted_iota(jnp.int32, (c, c), 1)
    tril = col <= row
    strict = col < row
    eye = (col == row).astype(F32)

    logits = ba_ref[...]
    zz = logits + dtb_ref[...]
    softplus = jnp.maximum(zz, 0.0) + jnp.log1p(jnp.exp(-jnp.abs(zz)))
    g_cum = jnp.dot(ltri_ref[...], -jnp.exp(alog_ref[...]) * softplus,
                    precision=HIGHEST, preferred_element_type=F32)
    lane = lax.broadcasted_iota(jnp.int32, logits.shape, 1)
    per_head = jnp.where(lane < B_HEADS, jax.nn.sigmoid(logits), g_cum).T
    gc_t = [jnp.broadcast_to(per_head[B_HEADS + hh:B_HEADS + hh + 1, :], (c, c)) for hh in heads]
    gc = [x.T for x in gc_t]
    beta = [jnp.broadcast_to(per_head[hh:hh + 1, :], (c, c)).T for hh in heads]

    buf[halo:halo + c, :] = x_ref[...]

    def conv(c0):
        cs = slice(c0, c0 + LANES)
        acc = buf[pl.ds(halo, c), cs] * cw_ref[B_CONV - 1:B_CONV, cs]
        for s in range(1, B_CONV):
            acc = acc + buf[pl.ds(halo - s, c), cs] * cw_ref[B_CONV - 1 - s:B_CONV - s, cs]
        return _silu(acc)

    def l2n(a):
        return a * lax.rsqrt(jnp.sum(a * a, axis=-1, keepdims=True) + EPS)

    q = [l2n(conv(hh * LANES)) * (B_DK ** -0.5) for hh in heads]
    k = [l2n(conv(B_WIDTH + hh * LANES)) for hh in heads]
    v = [conv(2 * B_WIDTH + hh * LANES) for hh in heads]
    buf[0:halo, :] = buf[c:c + halo, :]

    k_bf = [_bf(x) for x in k]
    kb = [x * bb for x, bb in zip(k, beta)]
    kq = [_dot_nt(jnp.concatenate([_bf(x), _bf(y)], axis=0), kk) for x, y, kk in zip(kb, q, k_bf)]
    decay = [jnp.exp(jnp.where(tril, g - gt, -jnp.inf)) for g, gt in zip(gc, gc_t)]
    a = [jnp.where(strict, x[:c] * dd, 0.0) for x, dd in zip(kq, decay)]
    qk_bf = [_bf(x[c:] * dd) for x, dd in zip(kq, decay)]
    tm = _unit_lower_inverse(a, eye, row, col)
    e_g = [jnp.exp(g) for g in gc]
    uw = [_dot(_bf(t), jnp.concatenate([_bf(vv * bb), _bf(x * eg)], axis=1))
          for t, vv, bb, x, eg in zip(tm, v, beta, kb, e_g)]
    s = [s_scr[hh] for hh in heads]
    s_bf = [_bf(x) for x in s]
    ws = [_dot(jnp.concatenate([_bf(x[:, LANES:]), _bf(qq * eg)], axis=0), sb)
          for x, qq, eg, sb in zip(uw, q, e_g, s_bf)]
    v_bf = [_bf(x[:, :LANES] - y[:c]) for x, y in zip(uw, ws)]
    o = [y[c:] + _dot(qq, vv) for y, qq, vv in zip(ws, qk_bf, v_bf)]
    g_last = [g[c - 1:c, :] for g in gc]
    kd_bf = [_bf(kk * jnp.exp(gl - g)) for kk, gl, g in zip(k, g_last, gc)]
    for hh in heads:
        sl = slice(hh * LANES, (hh + 1) * LANES)
        s_scr[hh] = s[hh] * jnp.exp(g_last[hh]) + _dot_tn(kd_bf[hh], v_bf[hh])
        o_ref[:, sl] = _gated_norm(o[hh], z_ref[:, sl], nw_ref[...]).astype(BF16)

    @pl.when(j == pl.num_programs(0) - 1)
    def _():
        sfin_ref[...] = s_scr[...]


def _delta_prompt(proj, ba, alog_row, dtb_row, ltri, cw, nw, t, x_col0, z_col0):
    c = CHUNK
    const = lambda shape: pl.BlockSpec(shape, lambda j: (0,) * len(shape))
    return pl.pallas_call(
        _delta_prompt_kernel,
        grid=(t // c,),
        in_specs=[
            pl.BlockSpec((pl.Element(c), pl.Element(B_CONV_COLS)),
                         lambda j: (pl.multiple_of(j * c, c), x_col0)),
            pl.BlockSpec((pl.Element(c), pl.Element(B_WIDTH)),
                         lambda j: (pl.multiple_of(j * c, c), z_col0)),
            pl.BlockSpec((c, LANES), lambda j: (j, 0)),
            const((1, LANES)), const((1, LANES)), const((c, c)), const((B_CONV, B_CONV_COLS)),
            const((1, LANES)),
        ],
        out_specs=[pl.BlockSpec((c, B_WIDTH), lambda j: (j, 0)),
                   const((B_HEADS, B_DK, B_DV))],
        out_shape=[jax.ShapeDtypeStruct((t, B_WIDTH), BF16),
                   jax.ShapeDtypeStruct((B_HEADS, B_DK, B_DV), F32)],
        scratch_shapes=[pltpu.VMEM((B_HEADS, B_DK, B_DV), F32),
                        pltpu.VMEM((c + SUBLANES, B_CONV_COLS), F32)],
        compiler_params=_cparams(("arbitrary",), 40),
        name="delta_prompt",
    )(proj, proj, ba, alog_row, dtb_row, ltri, cw, nw)


def _conv_sample_kernel(x_ref, s0_ref, s1_ref, s2_ref, cw_ref, o_ref):
    c = pl.program_id(0)
    acc = (s0_ref[...] * cw_ref[0:1, :] + s1_ref[...] * cw_ref[1:2, :]
           + s2_ref[...] * cw_ref[2:3, :] + x_ref[...] * cw_ref[3:4, :])
    y = _silu(acc)
    n_qk = 2 * B_HEADS // HEAD_GROUP

    @pl.when(c < n_qk // 2)
    def _():
        o_ref[...] = _l2norm_heads(y) * (B_DK ** -0.5)

    @pl.when(jnp.logical_and(c >= n_qk // 2, c < n_qk))
    def _():
        o_ref[...] = _l2norm_heads(y)

    @pl.when(c >= n_qk)
    def _():
        o_ref[...] = y


def _conv_sample(proj, state, cw, t, b, col0):
    ncb = B_CONV_COLS // COL_BLOCK
    cb0 = col0 // COL_BLOCK
    rb0 = t // b
    sspec = pl.BlockSpec((b, COL_BLOCK), lambda c: (0, c))
    return pl.pallas_call(
        _conv_sample_kernel,
        grid=(ncb,),
        in_specs=[pl.BlockSpec((b, COL_BLOCK), lambda c: (rb0, cb0 + c)), sspec, sspec, sspec,
                  pl.BlockSpec((B_CONV, COL_BLOCK), lambda c: (0, c))],
        out_specs=pl.BlockSpec((b, COL_BLOCK), lambda c: (0, c)),
        out_shape=jax.ShapeDtypeStruct((b, B_CONV_COLS), F32),
        compiler_params=_cparams(("arbitrary",), 32),
        name="conv_sample",
    )(proj, state[:, 0], state[:, 1], state[:, 2], cw)


def _delta_sample_kernel(q_ref, k_ref, v_ref, beta_ref, g_ref, z_ref, nw_ref, s_ref, o_ref, snew_ref, *, bb):
    row = lax.broadcasted_iota(jnp.int32, (LANES, LANES), 0)
    col = lax.broadcasted_iota(jnp.int32, (LANES, LANES), 1)
    eye = col == row

    def column(r):
        return jnp.sum(jnp.where(eye, r, 0.0), axis=-1, keepdims=True)

    units = [(b, hh, slice(hh * LANES, (hh + 1) * LANES)) for b in range(bb) for hh in range(HEAD_GROUP)]
    rows = [(q_ref[b:b + 1, sl], k_ref[b:b + 1, sl]) for b, _, sl in units]
    cols = [(column(q), column(k)) for q, k in rows]
    qk = [jnp.sum(q * k, axis=-1, keepdims=True) for q, k in rows]
    reads = [(jnp.sum(s_ref[b, hh] * qc, axis=0, keepdims=True),
              jnp.sum(s_ref[b, hh] * kc, axis=0, keepdims=True))
             for (b, hh, _), (qc, kc) in zip(units, cols)]
    outs = []
    for (b, hh, sl), (_, k_col), (qs, ks), qk_u in zip(units, cols, reads, qk):
        e_g = jnp.exp(g_ref[b:b + 1, sl])
        v_new = beta_ref[b:b + 1, sl] * (v_ref[b:b + 1, sl] - e_g * ks)
        outs.append(e_g * qs + qk_u * v_new)
        snew_ref[b, hh] = s_ref[b, hh] * e_g + k_col * v_new
    for (b, _, sl), o in zip(units, outs):
        o_ref[b:b + 1, sl] = _gated_norm(o, z_ref[b:b + 1, sl], nw_ref[...])


def _delta_sample(qkv, beta, g, proj, nw, state, t, b, z_col0):
    bb = SUBLANES
    hgs = B_HEADS // HEAD_GROUP
    zb0 = z_col0 // COL_BLOCK
    rb0 = t // bb
    blk = lambda off: pl.BlockSpec((bb, COL_BLOCK), lambda i, h, off=off: (i, off + h))
    sspec = pl.BlockSpec((bb, HEAD_GROUP, B_DK, B_DV), lambda i, h: (i, h, 0, 0))
    return pl.pallas_call(
        functools.partial(_delta_sample_kernel, bb=bb),
        grid=(b // bb, hgs),
        in_specs=[blk(0), blk(hgs), blk(2 * hgs), blk(0), blk(0),
                  pl.BlockSpec((bb, COL_BLOCK), lambda i, h: (rb0 + i, zb0 + h)),
                  pl.BlockSpec((1, LANES), lambda i, h: (0, 0)), sspec],
        out_specs=[pl.BlockSpec((bb, COL_BLOCK), lambda i, h: (i, h)), sspec],
        out_shape=[jax.ShapeDtypeStruct((b, B_WIDTH), F32),
                   jax.ShapeDtypeStruct(state.shape, F32)],
        compiler_params=_cparams(("arbitrary", "arbitrary"), 32),
        name="delta_sample",
    )(qkv, qkv, qkv, beta, g, proj, nw, state)


def _ffn_kernel(h_ref, hs_ref, wg_ref, wu_ref, cw_ref, s0_ref, s1_ref, act_ref, tail_ref, gs_ref,
                wg_bf, wu_bf, buf, *, tm, bs):
    i = pl.program_id(1)
    halo = SUBLANES

    @pl.when(i == 0)
    def _():
        wg_bf[...] = _bf(wg_ref[...])
        wu_bf[...] = _bf(wu_ref[...])
        buf[...] = jnp.zeros(buf.shape, F32)
        h = hs_ref[...]
        g = _dot(h, wg_bf[...])
        gs_ref[...] = g
        acc = s0_ref[...] * cw_ref[0:1, :] + s1_ref[...] * cw_ref[1:2, :] + g * cw_ref[2:3, :]
        act_ref[0:bs, :] = _bf(_silu(acc) * _dot(h, wu_bf[...]))

    @pl.when(i > 0)
    def _():
        rows = tm // ROW_PARTS
        parts = [slice(p * rows, (p + 1) * rows) for p in range(ROW_PARTS)]
        gu = [(_dot(h_ref[sl, :], wg_bf[...]), _dot(h_ref[sl, :], wu_bf[...])) for sl in parts]
        prev = buf[...]
        first_rows = lax.broadcasted_iota(jnp.int32, prev.shape, 0)
        for sl, (g, u) in zip(parts, gu):
            acc = g * cw_ref[FFN_CONV - 1:FFN_CONV, :]
            for s in range(1, FFN_CONV):
                rolled = pltpu.roll(g, s, 0)
                head = jnp.where(first_rows < s, pltpu.roll(prev, s, 0), rolled[0:halo])
                shifted = jnp.concatenate([head, rolled[halo:]], axis=0)
                acc = acc + shifted * cw_ref[FFN_CONV - 1 - s:FFN_CONV - s, :]
            prev = g[rows - halo:rows]
            act_ref[sl, :] = _bf(_silu(acc) * u)
        buf[...] = prev
        tail_ref[...] = prev


def _ffn(h2, wg, wu, cw, state, t, b):
    m, d = h2.shape
    f = wg.shape[1]
    tf = 256
    tm = _row_tile(t, 2048)
    npt = t // tm
    assert b <= tm and m == t + b and t % b == 0
    sspec = pl.BlockSpec((b, tf), lambda c, i: (0, c))
    nf = f // tf

    def weight_spec(switch_step):
        return pl.BlockSpec(
            (d, tf), lambda c, i: (0, jnp.minimum(c + (i >= switch_step).astype(jnp.int32), nf - 1)))

    return pl.pallas_call(
        functools.partial(_ffn_kernel, tm=tm, bs=b),
        grid=(nf, npt + 1),
        in_specs=[
            pl.BlockSpec((tm, d), lambda c, i: (jnp.maximum(i - 1, 0), 0)),
            pl.BlockSpec((b, d), lambda c, i: (t // b, 0)),
            weight_spec(2),
            weight_spec(min(4, npt)),
            pl.BlockSpec((FFN_CONV, tf), lambda c, i: (0, c)),
            sspec, sspec,
        ],
        out_specs=[pl.BlockSpec((tm, tf), lambda c, i: (jnp.where(i == 0, npt, i - 1), c)),
                   pl.BlockSpec((SUBLANES, tf), lambda c, i: (0, c)),
                   pl.BlockSpec((b, tf), lambda c, i: (0, c))],
        out_shape=[jax.ShapeDtypeStruct((m, f), BF16),
                   jax.ShapeDtypeStruct((SUBLANES, f), F32),
                   jax.ShapeDtypeStruct((b, f), F32)],
        scratch_shapes=[pltpu.VMEM((d, tf), BF16), pltpu.VMEM((d, tf), BF16),
                        pltpu.VMEM((SUBLANES, tf), F32)],
        compiler_params=_cparams(("arbitrary", "arbitrary"), 60),
        name="ffn",
    )(h2, h2, wg, wu, cw, state[:, 0], state[:, 1])


def _rope_tables(t, b):
    inv = ROPE_THETA ** (-jnp.arange(0, ROT_DIM, 2, dtype=F32) / ROT_DIM)
    pos = jnp.concatenate([jnp.arange(t, dtype=jnp.int32),
                           jnp.full((b,), PAST_LEN, jnp.int32)]).astype(F32)
    ang = pos[:, None] * inv[None, :]
    cos, sin = jnp.cos(ang), jnp.sin(ang)
    m = t + b
    pad = LANES - ROT_DIM
    cos_t = jnp.concatenate([cos, cos, jnp.ones((m, pad), F32)], axis=1)
    sin_t = jnp.concatenate([-sin, sin, jnp.zeros((m, pad), F32)], axis=1)
    return cos_t, sin_t


def kernel(x_prompt, x_sample, cache_kv_w128, cache_kv_w512, cache_kv_w2048, state_conv_qkv, state_delta,
           state_ffn_conv, norm_mix, w_in, conv_qkv, a_log, dt_bias, delta_norm, w_a_out, w_b_out, w_out,
           norm_ffn, w_gate, ffn_conv, w_up, w_down, norm_final):
    assert norm_mix.shape[0] == 1, "single layer only"
    assert x_prompt.shape[0] == 1 and x_sample.shape[1] == 1
    t, d = x_prompt.shape[1], x_prompt.shape[2]
    b = x_sample.shape[0]
    assert t % SUPER == 0 and b == ROW_BLOCK
    caches = (cache_kv_w128[0], cache_kv_w512[0], cache_kv_w2048[0])
    for (win, _), c in zip(DIL_GROUPS, caches):
        assert c.shape[1] == win, "cache must hold a full window"

    off_bqkv = A_COLS
    off_bz = off_bqkv + B_CONV_COLS
    off_ba = off_bz + B_WIDTH
    off_gate = off_ba + 2 * B_HEADS
    assert w_in.shape[2] == off_gate + 2 * d
    w_in_t = jnp.swapaxes(w_in[0], 0, 1)
    w_main = _bf(w_in_t)
    w_ba = jnp.pad(w_in_t[off_ba:off_gate], ((0, LANES - 2 * B_HEADS), (0, 0)))
    w_ba_hi = _bf(w_ba)
    w_ba = jnp.concatenate([w_ba_hi, _bf(w_ba - w_ba_hi.astype(F32))], axis=0)
    gate_col0 = off_ba

    xp = x_prompt.reshape(t, d)
    xs = x_sample.reshape(b, d)
    h, x_cat, ba = _norm_in(xp, xs, norm_mix, w_ba)
    cos_t, sin_t = _rope_tables(t, b)
    proj = _proj(h, w_main, cos_t, sin_t, gate_col0, off_gate)

    oa_p = _attn_prompt(proj, t)
    oa_s = _attn_sample(proj, caches, t, b)
    o_a = jnp.concatenate([oa_p, _bf(oa_s)], axis=0)

    alog_row = jnp.pad(a_log[0], (B_HEADS, LANES - 2 * B_HEADS)).reshape(1, LANES)
    dtb_row = jnp.pad(dt_bias[0], (B_HEADS, LANES - 2 * B_HEADS)).reshape(1, LANES)
    ltri = jnp.tril(jnp.ones((CHUNK, CHUNK), F32))
    head_of_col = jnp.arange(2 * B_WIDTH, dtype=jnp.int32) // LANES
    expand = (jnp.arange(LANES, dtype=jnp.int32)[:, None] == head_of_col[None, :]).astype(F32)
    cw_qkv = conv_qkv[0]
    ob_p, delta_p = _delta_prompt(proj, ba, alog_row, dtb_row, ltri, cw_qkv, delta_norm, t, off_bqkv, off_bz)
    beta_s, g_s = _beta_decay(ba, alog_row, dtb_row, ltri, expand, t, b, False)
    qkv_s = _conv_sample(proj, state_conv_qkv[0], cw_qkv, t, b, off_bqkv)
    ob_s, delta_s = _delta_sample(qkv_s, beta_s, g_s, proj, delta_norm, state_delta[0], t, b, off_bz)
    o_b = jnp.concatenate([ob_p, _bf(ob_s)], axis=0)

    merged = _merge(o_a, o_b, _bf(w_a_out[0]), _bf(w_b_out[0]), proj, gate_col0)
    x1 = _mm_res(merged, _bf(w_out[0]), x_cat, ROW_TILE_CAP, 512, "out_proj")

    h2 = _norm_mid(x1, norm_ffn)
    act, g_tail, g_smp = _ffn(h2, w_gate[0], w_up[0], ffn_conv[0], state_ffn_conv[0], t, b)
    x2 = _mm_res(act, _bf(w_down[0]), x1, 640, 512, "down_proj")
    y_p, y_s = _norm_out(x2, norm_final.reshape(1, d), t, b)

    kv_p, kv_s = [], []
    for g, (win, _) in enumerate(DIL_GROUPS):
        c0 = g * 3 * A_WIDTH + A_WIDTH
        keep = min(win, t)
        kv_p.append(proj[t - keep:t, c0:c0 + 2 * A_WIDTH].reshape(1, 1, keep, 2, A_HEADS, HEAD_DIM))
        kv_s.append(proj[t:, c0:c0 + 2 * A_WIDTH].reshape(1, b, 1, 2, A_HEADS, HEAD_DIM))
    conv_p = proj[t - (B_CONV - 1):t, off_bqkv:off_bz].reshape(1, 1, B_CONV - 1, B_CONV_COLS)
    conv_s = jnp.concatenate([state_conv_qkv[0][:, 1:], proj[t:, off_bqkv:off_bz][:, None]], axis=1)[None]
    ffn_p = g_tail[SUBLANES - (FFN_CONV - 1):].reshape(1, 1, FFN_CONV - 1, -1)
    ffn_s = jnp.concatenate([state_ffn_conv[0][:, 1:], g_smp[:, None]], axis=1)[None]
    return (y_p.reshape(1, t, d), y_s.reshape(b, 1, d), kv_p[0], kv_p[1], kv_p[2],
            conv_p, delta_p.reshape(1, 1, B_HEADS, B_DK, B_DV), ffn_p,
            kv_s[0], kv_s[1], kv_s[2], conv_s, delta_s[None], ffn_s)
```

```python
import functools

import jax
import jax.numpy as jnp
from jax import lax
from jax.experimental import pallas as pl
from jax.experimental.pallas import tpu as pltpu

F32 = jnp.float32
BF16 = jnp.bfloat16
HIGHEST = lax.Precision.HIGHEST

HEAD_DIM = 128
ROT_DIM = HEAD_DIM // 4
ROT_HALF = ROT_DIM // 2
ROPE_THETA = 500000.0
DIL_GROUPS = ((128, 1), (512, 4), (2048, 16))
N_GROUPS = len(DIL_GROUPS)
A_HEADS = 4
A_WIDTH = A_HEADS * HEAD_DIM
A_COLS = N_GROUPS * 3 * A_WIDTH
KEYS_PER_QUERY = 128
B_HEADS = 16
B_DK = 128
B_DV = 128
B_WIDTH = B_HEADS * B_DK
B_CONV = 4
B_CONV_COLS = 3 * B_WIDTH
FFN_CONV = 3
EPS = 1e-6
PAST_LEN = 2048

LANES = 128
SUBLANES = 8
ROW_BLOCK = 128
SUPER = 2048
CHUNK = 128
HEAD_GROUP = 4
COL_BLOCK = HEAD_GROUP * LANES
VMEM_CAP_MB = 60
MAX_ROW_PARTS = 5
ROW_TILE_CAP = 2080
BF16_ROWS = 16


def _row_parts(tm):
    n = max(p for p in range(1, MAX_ROW_PARTS + 1) if tm % (p * BF16_ROWS) == 0)
    return [slice(p * (tm // n), (p + 1) * (tm // n)) for p in range(n)]


def _cparams(sem, vmem_mb):
    return pltpu.CompilerParams(dimension_semantics=sem,
                                vmem_limit_bytes=int(min(vmem_mb, VMEM_CAP_MB) * 2 ** 20))


def _bf(x):
    return x.astype(BF16)


def _dot(a, b):
    return jnp.dot(a, b, preferred_element_type=F32)


def _dot_nt(a, b):
    return lax.dot_general(a, b, (((1,), (1,)), ((), ())), preferred_element_type=F32)


def _dot_tn(a, b):
    return lax.dot_general(a, b, (((0,), (0,)), ((), ())), preferred_element_type=F32)


def _silu(x):
    return x * jax.nn.sigmoid(x)


NORM_ROWS = 256


def _norm_in_kernel(xp_ref, xs_ref, nw_ref, wba_ref, h_ref, ba_ref, *, n_prompt_blocks, bs):
    i = pl.program_id(0)

    def body(x, rows):
        y = x * lax.rsqrt(jnp.mean(x * x, axis=-1, keepdims=True) + EPS) * nw_ref[...]
        y_hi = _bf(y)
        y_lo = _bf(y - y_hi.astype(F32))
        h_ref[rows, :] = y_hi
        hi = _dot_nt(y_hi, wba_ref[...])
        ba_ref[rows, :] = hi[:, :LANES] + hi[:, LANES:] + _dot_nt(y_lo, wba_ref[0:LANES, :])

    @pl.when(i < n_prompt_blocks)
    def _():
        body(xp_ref[...], slice(None))

    @pl.when(i == n_prompt_blocks)
    def _():
        body(xs_ref[...], slice(0, bs))


def _norm_in(xp, xs, nw, wba):
    t, d = xp.shape
    b = xs.shape[0]
    rb = NORM_ROWS
    npb = t // rb
    m = t + b
    return pl.pallas_call(
        functools.partial(_norm_in_kernel, n_prompt_blocks=npb, bs=b),
        grid=(npb + 1,),
        in_specs=[
            pl.BlockSpec((rb, d), lambda i: (jnp.minimum(i, npb - 1), 0)),
            pl.BlockSpec((b, d), lambda i: (0, 0)),
            pl.BlockSpec((1, d), lambda i: (0, 0)),
            pl.BlockSpec((2 * LANES, d), lambda i: (0, 0)),
        ],
        out_specs=[
            pl.BlockSpec((rb, d), lambda i: (i, 0)),
            pl.BlockSpec((rb, LANES), lambda i: (i, 0)),
        ],
        out_shape=[
            jax.ShapeDtypeStruct((m, d), BF16),
            jax.ShapeDtypeStruct((m, LANES), F32),
        ],
        compiler_params=_cparams(("arbitrary",), 40),
        name="norm_in",
    )(xp, xs, nw, wba)


def _norm_mid_kernel(x_ref, nw_ref, h_ref):
    x = x_ref[...]
    y = x * lax.rsqrt(jnp.mean(x * x, axis=-1, keepdims=True) + EPS) * nw_ref[...]
    h_ref[...] = _bf(y)


def _norm_mid(x, nw):
    m, d = x.shape
    rb = _row_tile(m, 640)
    return pl.pallas_call(
        _norm_mid_kernel,
        grid=(m // rb,),
        in_specs=[pl.BlockSpec((rb, d), lambda i: (i, 0)),
                  pl.BlockSpec((1, d), lambda i: (0, 0))],
        out_specs=pl.BlockSpec((rb, d), lambda i: (i, 0)),
        out_shape=jax.ShapeDtypeStruct((m, d), BF16),
        compiler_params=_cparams(("arbitrary",), 40),
        name="norm_mid",
    )(x, nw)


def _norm_out_kernel(x_ref, nw_ref, yp_ref, ys_ref, *, n_prompt_blocks, bs):
    i = pl.program_id(0)

    def norm(x):
        return x * lax.rsqrt(jnp.mean(x * x, axis=-1, keepdims=True) + EPS) * nw_ref[...]

    @pl.when(i < n_prompt_blocks)
    def _():
        yp_ref[...] = norm(x_ref[...])

    @pl.when(i == n_prompt_blocks)
    def _():
        ys_ref[...] = norm(x_ref[0:bs, :])


def _norm_out(x, nw, t, b):
    d = x.shape[1]
    rb = NORM_ROWS
    npb = t // rb
    return pl.pallas_call(
        functools.partial(_norm_out_kernel, n_prompt_blocks=npb, bs=b),
        grid=(npb + 1,),
        in_specs=[pl.BlockSpec((rb, d), lambda i: (i, 0)),
                  pl.BlockSpec((1, d), lambda i: (0, 0))],
        out_specs=[
            pl.BlockSpec((rb, d), lambda i: (jnp.minimum(i, npb - 1), 0)),
            pl.BlockSpec((b, d), lambda i: (0, 0)),
        ],
        out_shape=[jax.ShapeDtypeStruct((t, d), F32), jax.ShapeDtypeStruct((b, d), F32)],
        compiler_params=_cparams(("arbitrary",), 32),
        name="norm_out",
    )(x, nw)


def _row_tile(m, cap):
    best = 16
    for c in range(16, cap + 1, 16):
        if m % c == 0:
            best = c
    return best


def _proj_kernel(h_ref, w_ref, cos_ref, sin_ref, o_ref, *, n_attn_tiles, gate_tile0):
    j = pl.program_id(1)
    is_rope = jnp.logical_and(j < n_attn_tiles, j % 3 != 2)
    is_gate = j >= gate_tile0

    parts = _row_parts(h_ref.shape[0])

    def products():
        return [_dot_nt(h_ref[sl, :], w_ref[...]) for sl in parts]

    @pl.when(is_rope)
    def _():
        for sl, acc in zip(parts, products()):
            cos = cos_ref[sl, :]
            sin = sin_ref[sl, :]
            lane = lax.broadcasted_iota(jnp.int32, cos.shape, 1)
            for hh in range(A_HEADS):
                a = acc[:, hh * HEAD_DIM:(hh + 1) * HEAD_DIM]
                partner = jnp.where(lane < ROT_HALF,
                                    pltpu.roll(a, HEAD_DIM - ROT_HALF, 1),
                                    pltpu.roll(a, ROT_HALF, 1))
                o_ref[sl, hh * HEAD_DIM:(hh + 1) * HEAD_DIM] = a * cos + partner * sin

    @pl.when(is_gate)
    def _():
        for sl, acc in zip(parts, products()):
            o_ref[sl, :] = jax.nn.sigmoid(acc)

    @pl.when(jnp.logical_not(jnp.logical_or(is_rope, is_gate)))
    def _():
        for sl, acc in zip(parts, products()):
            o_ref[sl, :] = acc


def _proj(h, w, cos, sin, gate_col0, gate_row0):
    m, k = h.shape
    skip = gate_row0 - gate_col0
    n = w.shape[0] - skip
    tn = A_WIDTH
    tm = _row_tile(m, ROW_TILE_CAP)
    gate_tile0 = gate_col0 // tn
    return pl.pallas_call(
        functools.partial(_proj_kernel, n_attn_tiles=A_COLS // tn, gate_tile0=gate_tile0),
        grid=(m // tm, n // tn),
        in_specs=[
            pl.BlockSpec((tm, k), lambda i, j: (i, 0)),
            pl.BlockSpec((pl.Element(tn), pl.Element(k)),
                         lambda i, j: (pl.multiple_of(j * tn + jnp.where(j >= gate_tile0, skip, 0),
                                                      2 * SUBLANES), 0)),
            pl.BlockSpec((tm, LANES), lambda i, j: (i, 0)),
            pl.BlockSpec((tm, LANES), lambda i, j: (i, 0)),
        ],
        out_specs=pl.BlockSpec((tm, tn), lambda i, j: (i, j)),
        out_shape=jax.ShapeDtypeStruct((m, n), F32),
        compiler_params=_cparams(("arbitrary", "arbitrary"), 60),
        name="proj_in",
    )(h, w, cos, sin)


def _merge_kernel(oap_ref, oas_ref, obp_ref, obs_ref, wa_ref, wb_ref, ga_ref, gb_ref, o_ref,
                  *, n_prompt_tiles, bs):
    i = pl.program_id(0)

    def body(oa, ob, rows):
        a = _dot(oa, wa_ref[...])
        b = _dot(ob, wb_ref[...])
        o_ref[rows, :] = _bf(ga_ref[rows, :] * a + gb_ref[rows, :] * b)

    @pl.when(i < n_prompt_tiles)
    def _():
        body(oap_ref[...], obp_ref[...], slice(None))

    @pl.when(i == n_prompt_tiles)
    def _():
        body(_bf(oas_ref[...]), _bf(obs_ref[...]), slice(0, bs))


def _merge(oa_p, oa_s, ob_p, ob_s, wa, wb, proj, gate_col0):
    t, b = oa_p.shape[0], oa_s.shape[0]
    n = wa.shape[1]
    tn = COL_BLOCK
    tm = _row_tile(t, 2048)
    npt = t // tm
    ga0 = gate_col0 // tn
    gb0 = (gate_col0 + n) // tn
    prompt = lambda width: pl.BlockSpec((tm, width), lambda i, j: (jnp.minimum(i, npt - 1), 0))
    sample = lambda width: pl.BlockSpec((b, width), lambda i, j: (0, 0))
    return pl.pallas_call(
        functools.partial(_merge_kernel, n_prompt_tiles=npt, bs=b),
        grid=(npt + 1, n // tn),
        in_specs=[
            prompt(oa_p.shape[1]), sample(oa_s.shape[1]), prompt(ob_p.shape[1]), sample(ob_s.shape[1]),
            pl.BlockSpec((wa.shape[0], tn), lambda i, j: (0, j)),
            pl.BlockSpec((wb.shape[0], tn), lambda i, j: (0, j)),
            pl.BlockSpec((tm, tn), lambda i, j: (i, ga0 + j)),
            pl.BlockSpec((tm, tn), lambda i, j: (i, gb0 + j)),
        ],
        out_specs=pl.BlockSpec((tm, tn), lambda i, j: (i, j)),
        out_shape=jax.ShapeDtypeStruct((t + b, n), BF16),
        compiler_params=_cparams(("arbitrary", "arbitrary"), 58),
        name="merge",
    )(oa_p, oa_s, ob_p, ob_s, wa, wb, proj, proj)


def _out_proj_kernel(a_ref, w_ref, rp_ref, rs_ref, o_ref, *, n_prompt_tiles, bs):
    i = pl.program_id(0)

    @pl.when(i < n_prompt_tiles)
    def _():
        o_ref[...] = rp_ref[...] + _dot(a_ref[...], w_ref[...])

    @pl.when(i == n_prompt_tiles)
    def _():
        o_ref[0:bs, :] = rs_ref[...] + _dot(a_ref[0:bs, :], w_ref[...])


def _out_proj(a, w, res_p, res_s):
    m, k = a.shape
    t, b = res_p.shape[0], res_s.shape[0]
    n = w.shape[1]
    tn = COL_BLOCK
    tm = _row_tile(t, 1024)
    npt = t // tm
    return pl.pallas_call(
        functools.partial(_out_proj_kernel, n_prompt_tiles=npt, bs=b),
        grid=(npt + 1, n // tn),
        in_specs=[
            pl.BlockSpec((tm, k), lambda i, j: (i, 0)),
            pl.BlockSpec((k, tn), lambda i, j: (0, j)),
            pl.BlockSpec((tm, tn), lambda i, j: (jnp.minimum(i, npt - 1), j)),
            pl.BlockSpec((b, tn), lambda i, j: (0, j)),
        ],
        out_specs=pl.BlockSpec((tm, tn), lambda i, j: (i, j)),
        out_shape=jax.ShapeDtypeStruct((m, n), F32),
        compiler_params=_cparams(("arbitrary", "arbitrary"), 40),
        name="out_proj",
    )(a, w, res_p, res_s)


def _mm_res_kernel(a_ref, w_ref, r_ref, o_ref):
    o_ref[...] = r_ref[...] + _dot(a_ref[...], w_ref[...])


def _mm_res(a, w, res, tm_cap, tn, name):
    m, k = a.shape
    n = w.shape[1]
    tm = _row_tile(m, tm_cap)
    return pl.pallas_call(
        _mm_res_kernel,
        grid=(m // tm, n // tn),
        in_specs=[
            pl.BlockSpec((tm, k), lambda i, j: (i, 0)),
            pl.BlockSpec((k, tn), lambda i, j: (0, j)),
            pl.BlockSpec((tm, tn), lambda i, j: (i, j)),
        ],
        out_specs=pl.BlockSpec((tm, tn), lambda i, j: (i, j)),
        out_shape=jax.ShapeDtypeStruct((m, n), F32),
        compiler_params=_cparams(("arbitrary", "arbitrary"), 60),
        name=name,
    )(a, w, res)


ATTN_TILE_BATCH = 8


def _attn_prompt_kernel(*refs):
    grp = [refs[5 * g:5 * g + 5] for g in range(N_GROUPS)]
    o_ref, o_scr, l_scr = refs[5 * N_GROUPS:]
    first = pl.program_id(1) == 0
    row = lax.broadcasted_iota(jnp.int32, (LANES, LANES), 0)
    col = lax.broadcasted_iota(jnp.int32, (LANES, LANES), 1)
    mask_cur = col <= row
    mask_prev = col >= row
    mask_prev_first = col >= row + jnp.where(first, LANES, 0)
    scale = HEAD_DIM ** -0.5
    nq = KEYS_PER_QUERY

    def rows(dil, base, r):
        if dil == 1:
            return pl.ds(base, nq)
        return pl.ds(base + r, nq, stride=dil)

    tiles = [(g, dil, sb, r) for g, (_, dil) in enumerate(DIL_GROUPS)
             for sb in range(SUPER // (nq * dil)) for r in range(dil)]
    for t0 in range(0, len(tiles), ATTN_TILE_BATCH):
        batch = tiles[t0:t0 + ATTN_TILE_BATCH]
        scores = []
        for g, dil, sb, r in batch:
            q_ref, k_ref, v_ref, kp_ref, vp_ref = grp[g]
            span = nq * dil
            cur = rows(dil, sb * span, r)
            qt = _bf(q_ref[cur, :])
            if sb > 0:
                prev = rows(dil, (sb - 1) * span, r)
                kp, vp, mp = k_ref[prev, :], v_ref[prev, :], mask_prev
            else:
                prev = rows(dil, 0, r)
                kp, vp, mp = kp_ref[prev, :], vp_ref[prev, :], mask_prev_first
            s_c = jnp.where(mask_cur, _dot_nt(qt, _bf(k_ref[cur, :])) * scale, -jnp.inf)
            s_p = jnp.where(mp, _dot_nt(qt, _bf(kp)) * scale, -jnp.inf)
            scores.append((s_c, s_p, _bf(v_ref[cur, :]), _bf(vp)))
        probs = []
        for s_c, s_p, vc, vp in scores:
            mx = jnp.maximum(jnp.max(s_c, axis=-1, keepdims=True),
                             jnp.max(s_p, axis=-1, keepdims=True))
            p_c = jnp.exp(s_c - mx)
            p_p = jnp.exp(s_p - mx)
            den = jnp.sum(p_c, axis=-1, keepdims=True) + jnp.sum(p_p, axis=-1, keepdims=True)
            probs.append((_bf(p_c), _bf(p_p), vc, vp, den, mx))
        for (g, dil, sb, r), (p_c, p_p, vc, vp, den, mx) in zip(batch, probs):
            o = (_dot(p_c, vc) + _dot(p_p, vp)) / den
            dst = rows(dil, sb * nq * dil, r)
            o_scr[g, dst, :] = o
            l_scr[g, dst, :] = jnp.broadcast_to(mx + jnp.log(den), (nq, LANES))

    l0, l1, l2 = l_scr[0], l_scr[1], l_scr[2]
    mx = jnp.maximum(jnp.maximum(l0, l1), l2)
    e0, e1, e2 = jnp.exp(l0 - mx), jnp.exp(l1 - mx), jnp.exp(l2 - mx)
    o = (e0 * o_scr[0] + e1 * o_scr[1] + e2 * o_scr[2]) / (e0 + e1 + e2)
    o_ref[...] = _bf(o)


def _attn_prompt(proj, t):
    nsup = t // SUPER
    in_specs = []
    args = []
    for g, (_, dil) in enumerate(DIL_GROUPS):
        span = KEYS_PER_QUERY * dil
        per = SUPER // span
        cb = g * 3 * A_HEADS
        for part in range(3):
            in_specs.append(pl.BlockSpec(
                (SUPER, HEAD_DIM), lambda h, i, c=cb + part * A_HEADS: (i, c + h)))
            args.append(proj)
        for part in (1, 2):
            in_specs.append(pl.BlockSpec(
                (span, HEAD_DIM),
                lambda h, i, c=cb + part * A_HEADS, per=per: (jnp.maximum(i * per - 1, 0), c + h)))
            args.append(proj)
    return pl.pallas_call(
        _attn_prompt_kernel,
        grid=(A_HEADS, nsup),
        in_specs=in_specs,
        out_specs=pl.BlockSpec((SUPER, HEAD_DIM), lambda h, i: (i, h)),
        out_shape=jax.ShapeDtypeStruct((t, A_WIDTH), BF16),
        scratch_shapes=[pltpu.VMEM((N_GROUPS, SUPER, HEAD_DIM), F32),
                        pltpu.VMEM((N_GROUPS, SUPER, HEAD_DIM), F32)],
        compiler_params=_cparams(("arbitrary", "arbitrary"), 48),
        name="attn_prompt",
    )(*args)


def _attn_sample_kernel(qkv_ref, c0_ref, c1_ref, c2_ref, o_ref, *, bb):
    caches = (c0_ref, c1_ref, c2_ref)
    scale = HEAD_DIM ** -0.5
    sub = lax.broadcasted_iota(jnp.int32, (2 * A_HEADS, HEAD_DIM), 0)

    def on_sublanes(b, col0, first):
        out = jnp.zeros((2 * A_HEADS, HEAD_DIM), F32)
        for hh in range(A_HEADS):
            r = qkv_ref[b:b + 1, col0 + hh * HEAD_DIM:col0 + (hh + 1) * HEAD_DIM]
            out = jnp.where(sub == first + hh, r, out)
        return out

    for b in range(bb):
        outs, lses = [], []
        for g in range(N_GROUPS):
            c0 = g * 3 * A_WIDTH
            q8 = on_sublanes(b, c0, 0)
            kn8 = on_sublanes(b, c0 + A_WIDTH, 0)
            vn8 = on_sublanes(b, c0 + 2 * A_WIDTH, A_HEADS)
            kv = caches[g][b]
            s = jnp.sum(kv * q8, axis=-1, keepdims=True) * scale
            s = pltpu.roll(jnp.broadcast_to(s, kv.shape), A_HEADS, 1)
            sn = jnp.sum(kn8 * q8, axis=-1, keepdims=True) * scale
            sn = pltpu.roll(jnp.broadcast_to(sn, q8.shape), A_HEADS, 0)
            mx = jnp.maximum(jnp.max(s, axis=0), sn)
            p = jnp.exp(s - mx)
            pn = jnp.exp(sn - mx)
            den = jnp.sum(p, axis=0) + pn
            outs.append((jnp.sum(p * kv, axis=0) + pn * vn8) / den)
            lses.append(mx + jnp.log(den))
        mx = jnp.maximum(jnp.maximum(lses[0], lses[1]), lses[2])
        es = [jnp.exp(l - mx) for l in lses]
        o = (es[0] * outs[0] + es[1] * outs[1] + es[2] * outs[2]) / (es[0] + es[1] + es[2])
        for hh in range(A_HEADS):
            o_ref[b:b + 1, hh * HEAD_DIM:(hh + 1) * HEAD_DIM] = o[A_HEADS + hh:A_HEADS + hh + 1, :]


def _attn_sample(proj, caches, t, b):
    bb = SUBLANES
    in_specs = [pl.BlockSpec((bb, A_COLS), lambda i: (t // bb + i, 0))]
    args = [proj]
    for (win, dil), c in zip(DIL_GROUPS, caches):
        args.append(c.reshape(b, win // dil, dil, 2 * A_HEADS, HEAD_DIM))
        in_specs.append(pl.BlockSpec((bb, win // dil, None, 2 * A_HEADS, HEAD_DIM),
                                     lambda i: (i, 0, 0, 0, 0)))
    return pl.pallas_call(
        functools.partial(_attn_sample_kernel, bb=bb),
        grid=(b // bb,),
        in_specs=in_specs,
        out_specs=pl.BlockSpec((bb, A_WIDTH), lambda i: (i, 0)),
        out_shape=jax.ShapeDtypeStruct((b, A_WIDTH), F32),
        compiler_params=_cparams(("arbitrary",), 40),
        name="attn_sample",
    )(*args)


def _beta_decay_kernel(ba_ref, alog_ref, dtb_ref, ltri_ref, e_ref, beta_ref, gc_ref, *, cumulative):
    x = ba_ref[...]
    lane = lax.broadcasted_iota(jnp.int32, x.shape, 1)
    z = x + dtb_ref[...]
    softplus = jnp.maximum(z, 0.0) + jnp.log1p(jnp.exp(-jnp.abs(z)))
    g = -jnp.exp(alog_ref[...]) * softplus
    if cumulative:
        g = jnp.dot(ltri_ref[...], g, precision=HIGHEST, preferred_element_type=F32)
    vals = jnp.where(lane < B_HEADS, jax.nn.sigmoid(x), g)
    wide = jnp.dot(vals, e_ref[...], precision=HIGHEST, preferred_element_type=F32)
    beta_ref[...] = wide[:, :B_WIDTH]
    gc_ref[...] = wide[:, B_WIDTH:]


def _beta_decay(ba, alog_row, dtb_row, ltri, expand, row0, rows, cumulative):
    blk0 = row0 // CHUNK
    return pl.pallas_call(
        functools.partial(_beta_decay_kernel, cumulative=cumulative),
        grid=(rows // CHUNK,),
        in_specs=[
            pl.BlockSpec((CHUNK, LANES), lambda i: (blk0 + i, 0)),
            pl.BlockSpec((1, LANES), lambda i: (0, 0)),
            pl.BlockSpec((1, LANES), lambda i: (0, 0)),
            pl.BlockSpec((CHUNK, CHUNK), lambda i: (0, 0)),
            pl.BlockSpec((LANES, 2 * B_WIDTH), lambda i: (0, 0)),
        ],
        out_specs=[pl.BlockSpec((CHUNK, B_WIDTH), lambda i: (i, 0)),
                   pl.BlockSpec((CHUNK, B_WIDTH), lambda i: (i, 0))],
        out_shape=[jax.ShapeDtypeStruct((rows, B_WIDTH), F32),
                   jax.ShapeDtypeStruct((rows, B_WIDTH), F32)],
        compiler_params=_cparams(("arbitrary",), 32),
        name="beta_decay",
    )(ba, alog_row, dtb_row, ltri, expand)


def _l2norm_heads(x):
    parts = []
    for hh in range(x.shape[1] // HEAD_DIM):
        a = x[:, hh * HEAD_DIM:(hh + 1) * HEAD_DIM]
        parts.append(a * lax.rsqrt(jnp.sum(a * a, axis=-1, keepdims=True) + EPS))
    return jnp.concatenate(parts, axis=1)


def _gated_norm(o, z, nw):
    y = o * lax.rsqrt(jnp.mean(o * o, axis=-1, keepdims=True) + EPS) * nw
    return y * _silu(z)


INV_BASE = 16


assert CHUNK == LANES


def _unit_lower_inverse(a, eye, row, col):
    c = a[0].shape[0]

    def blk(size):
        sh = size.bit_length() - 1
        return (row >> sh) == (col >> sh)

    base = blk(INV_BASE)
    d = [jnp.where(base, x, 0.0) for x in a]
    tm = [eye - x for x in d]
    pw = [_bf(x) for x in d]
    for _ in range(INV_BASE.bit_length() - 2):
        pw = [_bf(_dot(x, x)) for x in pw]
        tm = [x + _dot(_bf(x), p) for x, p in zip(tm, pw)]
    size = INV_BASE
    while size < c:
        sel = jnp.logical_and(blk(2 * size), jnp.logical_not(blk(size)))
        tm_bf = [_bf(x) for x in tm]
        inner = [_bf(_dot(_bf(jnp.where(sel, x, 0.0)), y)) for x, y in zip(a, tm_bf)]
        tm = [x - _dot(y, z) for x, y, z in zip(tm, tm_bf, inner)]
        size *= 2
    return tm


def _delta_prompt_kernel(x_ref, z_ref, ba_ref, alog_ref, dtb_ref, ltri_ref, cw_ref, nw_ref,
                         o_ref, sfin_ref, s_scr, buf):
    j = pl.program_id(0)
    c = CHUNK
    halo = SUBLANES
    heads = range(B_HEADS)

    @pl.when(j == 0)
    def _():
        s_scr[...] = jnp.zeros_like(s_scr)
        buf[0:halo, :] = jnp.zeros((halo, buf.shape[1]), F32)

    row = lax.broadcasted_iota(jnp.int32, (c, c), 0)
    col = lax.broadcasted_iota(jnp.int32, (c, c), 1)
    tril = col <= row
    strict = col < row
    eye = (col == row).astype(F32)

    logits = ba_ref[...]
    zz = logits + dtb_ref[...]
    softplus = jnp.maximum(zz, 0.0) + jnp.log1p(jnp.exp(-jnp.abs(zz)))
    g_cum = jnp.dot(ltri_ref[...], -jnp.exp(alog_ref[...]) * softplus,
                    precision=HIGHEST, preferred_element_type=F32)
    lane = lax.broadcasted_iota(jnp.int32, logits.shape, 1)
    per_head = jnp.where(lane < B_HEADS, jax.nn.sigmoid(logits), g_cum).T
    gc_t = [jnp.broadcast_to(per_head[B_HEADS + hh:B_HEADS + hh + 1, :], (c, c)) for hh in heads]
    gc = [x.T for x in gc_t]
    beta = [jnp.broadcast_to(per_head[hh:hh + 1, :], (c, c)).T for hh in heads]

    buf[halo:halo + c, :] = x_ref[...]

    def conv(c0):
        cs = slice(c0, c0 + LANES)
        acc = buf[pl.ds(halo, c), cs] * cw_ref[B_CONV - 1:B_CONV, cs]
        for s in range(1, B_CONV):
            acc = acc + buf[pl.ds(halo - s, c), cs] * cw_ref[B_CONV - 1 - s:B_CONV - s, cs]
        return _silu(acc)

    def l2n(a):
        return a * lax.rsqrt(jnp.sum(a * a, axis=-1, keepdims=True) + EPS)

    q = [l2n(conv(hh * LANES)) * (B_DK ** -0.5) for hh in heads]
    k = [l2n(conv(B_WIDTH + hh * LANES)) for hh in heads]
    v = [conv(2 * B_WIDTH + hh * LANES) for hh in heads]
    buf[0:halo, :] = buf[c:c + halo, :]

    k_bf = [_bf(x) for x in k]
    kb = [x * bb for x, bb in zip(k, beta)]
    kq = [_dot_nt(jnp.concatenate([_bf(x), _bf(y)], axis=0), kk) for x, y, kk in zip(kb, q, k_bf)]
    decay = [jnp.exp(jnp.where(tril, g - gt, -jnp.inf)) for g, gt in zip(gc, gc_t)]
    a = [jnp.where(strict, x[:c] * dd, 0.0) for x, dd in zip(kq, decay)]
    qk_bf = [_bf(x[c:] * dd) for x, dd in zip(kq, decay)]
    tm = _unit_lower_inverse(a, eye, row, col)
    e_g = [jnp.exp(g) for g in gc]
    uw = [_dot(_bf(t), jnp.concatenate([_bf(vv * bb), _bf(x * eg)], axis=1))
          for t, vv, bb, x, eg in zip(tm, v, beta, kb, e_g)]
    s = [s_scr[hh] for hh in heads]
    s_bf = [_bf(x) for x in s]
    ws = [_dot(jnp.concatenate([_bf(x[:, LANES:]), _bf(qq * eg)], axis=0), sb)
          for x, qq, eg, sb in zip(uw, q, e_g, s_bf)]
    v_bf = [_bf(x[:, :LANES] - y[:c]) for x, y in zip(uw, ws)]
    o = [y[c:] + _dot(qq, vv) for y, qq, vv in zip(ws, qk_bf, v_bf)]
    g_last = [g[c - 1:c, :] for g in gc]
    kd_bf = [_bf(kk * jnp.exp(gl - g)) for kk, gl, g in zip(k, g_last, gc)]
    for hh in heads:
        sl = slice(hh * LANES, (hh + 1) * LANES)
        s_scr[hh] = s[hh] * jnp.exp(g_last[hh]) + _dot_tn(kd_bf[hh], v_bf[hh])
        o_ref[:, sl] = _gated_norm(o[hh], z_ref[:, sl], nw_ref[...]).astype(BF16)

    @pl.when(j == pl.num_programs(0) - 1)
    def _():
        sfin_ref[...] = s_scr[...]


def _delta_prompt(proj, ba, alog_row, dtb_row, ltri, cw, nw, t, x_col0, z_col0):
    c = CHUNK
    const = lambda shape: pl.BlockSpec(shape, lambda j: (0,) * len(shape))
    return pl.pallas_call(
        _delta_prompt_kernel,
        grid=(t // c,),
        in_specs=[
            pl.BlockSpec((pl.Element(c), pl.Element(B_CONV_COLS)),
                         lambda j: (pl.multiple_of(j * c, c), x_col0)),
            pl.BlockSpec((pl.Element(c), pl.Element(B_WIDTH)),
                         lambda j: (pl.multiple_of(j * c, c), z_col0)),
            pl.BlockSpec((c, LANES), lambda j: (j, 0)),
            const((1, LANES)), const((1, LANES)), const((c, c)), const((B_CONV, B_CONV_COLS)),
            const((1, LANES)),
        ],
        out_specs=[pl.BlockSpec((c, B_WIDTH), lambda j: (j, 0)),
                   const((B_HEADS, B_DK, B_DV))],
        out_shape=[jax.ShapeDtypeStruct((t, B_WIDTH), BF16),
                   jax.ShapeDtypeStruct((B_HEADS, B_DK, B_DV), F32)],
        scratch_shapes=[pltpu.VMEM((B_HEADS, B_DK, B_DV), F32),
                        pltpu.VMEM((c + SUBLANES, B_CONV_COLS), F32)],
        compiler_params=_cparams(("arbitrary",), 40),
        name="delta_prompt",
    )(proj, proj, ba, alog_row, dtb_row, ltri, cw, nw)


def _conv_sample_kernel(x_ref, s0_ref, s1_ref, s2_ref, cw_ref, o_ref):
    c = pl.program_id(0)
    acc = (s0_ref[...] * cw_ref[0:1, :] + s1_ref[...] * cw_ref[1:2, :]
           + s2_ref[...] * cw_ref[2:3, :] + x_ref[...] * cw_ref[3:4, :])
    y = _silu(acc)
    n_qk = 2 * B_HEADS // HEAD_GROUP

    @pl.when(c < n_qk // 2)
    def _():
        o_ref[...] = _l2norm_heads(y) * (B_DK ** -0.5)

    @pl.when(jnp.logical_and(c >= n_qk // 2, c < n_qk))
    def _():
        o_ref[...] = _l2norm_heads(y)

    @pl.when(c >= n_qk)
    def _():
        o_ref[...] = y


def _conv_sample(proj, state, cw, t, b, col0):
    ncb = B_CONV_COLS // COL_BLOCK
    cb0 = col0 // COL_BLOCK
    rb0 = t // b
    sspec = pl.BlockSpec((b, COL_BLOCK), lambda c: (0, c))
    return pl.pallas_call(
        _conv_sample_kernel,
        grid=(ncb,),
        in_specs=[pl.BlockSpec((b, COL_BLOCK), lambda c: (rb0, cb0 + c)), sspec, sspec, sspec,
                  pl.BlockSpec((B_CONV, COL_BLOCK), lambda c: (0, c))],
        out_specs=pl.BlockSpec((b, COL_BLOCK), lambda c: (0, c)),
        out_shape=jax.ShapeDtypeStruct((b, B_CONV_COLS), F32),
        compiler_params=_cparams(("arbitrary",), 32),
        name="conv_sample",
    )(proj, state[:, 0], state[:, 1], state[:, 2], cw)


def _delta_sample_kernel(q_ref, k_ref, v_ref, beta_ref, g_ref, z_ref, nw_ref, s_ref, o_ref, snew_ref, *, bb):
    row = lax.broadcasted_iota(jnp.int32, (LANES, LANES), 0)
    col = lax.broadcasted_iota(jnp.int32, (LANES, LANES), 1)
    eye = col == row

    def column(r):
        return jnp.sum(jnp.where(eye, r, 0.0), axis=-1, keepdims=True)

    units = [(b, hh, slice(hh * LANES, (hh + 1) * LANES)) for b in range(bb) for hh in range(HEAD_GROUP)]
    rows = [(q_ref[b:b + 1, sl], k_ref[b:b + 1, sl]) for b, _, sl in units]
    cols = [(column(q), column(k)) for q, k in rows]
    qk = [jnp.sum(q * k, axis=-1, keepdims=True) for q, k in rows]
    reads = [(jnp.sum(s_ref[b, hh] * qc, axis=0, keepdims=True),
              jnp.sum(s_ref[b, hh] * kc, axis=0, keepdims=True))
             for (b, hh, _), (qc, kc) in zip(units, cols)]
    outs = []
    for (b, hh, sl), (_, k_col), (qs, ks), qk_u in zip(units, cols, reads, qk):
        e_g = jnp.exp(g_ref[b:b + 1, sl])
        v_new = beta_ref[b:b + 1, sl] * (v_ref[b:b + 1, sl] - e_g * ks)
        outs.append(e_g * qs + qk_u * v_new)
        snew_ref[b, hh] = s_ref[b, hh] * e_g + k_col * v_new
    for (b, _, sl), o in zip(units, outs):
        o_ref[b:b + 1, sl] = _gated_norm(o, z_ref[b:b + 1, sl], nw_ref[...])


def _delta_sample(qkv, beta, g, proj, nw, state, t, b, z_col0):
    bb = SUBLANES
    hgs = B_HEADS // HEAD_GROUP
    zb0 = z_col0 // COL_BLOCK
    rb0 = t // bb
    blk = lambda off: pl.BlockSpec((bb, COL_BLOCK), lambda i, h, off=off: (i, off + h))
    sspec = pl.BlockSpec((bb, HEAD_GROUP, B_DK, B_DV), lambda i, h: (i, h, 0, 0))
    return pl.pallas_call(
        functools.partial(_delta_sample_kernel, bb=bb),
        grid=(b // bb, hgs),
        in_specs=[blk(0), blk(hgs), blk(2 * hgs), blk(0), blk(0),
                  pl.BlockSpec((bb, COL_BLOCK), lambda i, h: (rb0 + i, zb0 + h)),
                  pl.BlockSpec((1, LANES), lambda i, h: (0, 0)), sspec],
        out_specs=[pl.BlockSpec((bb, COL_BLOCK), lambda i, h: (i, h)), sspec],
        out_shape=[jax.ShapeDtypeStruct((b, B_WIDTH), F32),
                   jax.ShapeDtypeStruct(state.shape, F32)],
        compiler_params=_cparams(("arbitrary", "arbitrary"), 32),
        name="delta_sample",
    )(qkv, qkv, qkv, beta, g, proj, nw, state)


def _ffn_kernel(h_ref, hs_ref, wg_ref, wu_ref, cw_ref, s0_ref, s1_ref, act_ref, tail_ref, gs_ref,
                wg_bf, wu_bf, buf, *, tm, bs):
    i = pl.program_id(1)
    halo = SUBLANES

    @pl.when(i == 0)
    def _():
        wg_bf[...] = _bf(wg_ref[...])
        wu_bf[...] = _bf(wu_ref[...])
        buf[...] = jnp.zeros(buf.shape, F32)
        h = hs_ref[...]
        g = _dot(h, wg_bf[...])
        gs_ref[...] = g
        acc = s0_ref[...] * cw_ref[0:1, :] + s1_ref[...] * cw_ref[1:2, :] + g * cw_ref[2:3, :]
        act_ref[0:bs, :] = _bf(_silu(acc) * _dot(h, wu_bf[...]))

    @pl.when(i > 0)
    def _():
        parts = _row_parts(tm)
        rows = parts[0].stop
        gu = [(_dot(h_ref[sl, :], wg_bf[...]), _dot(h_ref[sl, :], wu_bf[...])) for sl in parts]
        prev = buf[...]
        first_rows = lax.broadcasted_iota(jnp.int32, prev.shape, 0)
        for sl, (g, u) in zip(parts, gu):
            acc = g * cw_ref[FFN_CONV - 1:FFN_CONV, :]
            for s in range(1, FFN_CONV):
                rolled = pltpu.roll(g, s, 0)
                head = jnp.where(first_rows < s, pltpu.roll(prev, s, 0), rolled[0:halo])
                shifted = jnp.concatenate([head, rolled[halo:]], axis=0)
                acc = acc + shifted * cw_ref[FFN_CONV - 1 - s:FFN_CONV - s, :]
            prev = g[rows - halo:rows]
            act_ref[sl, :] = _bf(_silu(acc) * u)
        buf[...] = prev
        tail_ref[...] = prev


def _ffn(h2, wg, wu, cw, state, t, b):
    m, d = h2.shape
    f = wg.shape[1]
    tf = 256
    tm = _row_tile(t, 2048)
    npt = t // tm
    assert b <= tm and m == t + b and t % b == 0
    sspec = pl.BlockSpec((b, tf), lambda c, i: (0, c))
    nf = f // tf

    def weight_spec(switch_step):
        return pl.BlockSpec(
            (d, tf), lambda c, i: (0, jnp.minimum(c + (i >= switch_step).astype(jnp.int32), nf - 1)))

    return pl.pallas_call(
        functools.partial(_ffn_kernel, tm=tm, bs=b),
        grid=(nf, npt + 1),
        in_specs=[
            pl.BlockSpec((tm, d), lambda c, i: (jnp.maximum(i - 1, 0), 0)),
            pl.BlockSpec((b, d), lambda c, i: (t // b, 0)),
            weight_spec(2),
            weight_spec(min(4, npt)),
            pl.BlockSpec((FFN_CONV, tf), lambda c, i: (0, c)),
            sspec, sspec,
        ],
        out_specs=[pl.BlockSpec((tm, tf), lambda c, i: (jnp.where(i == 0, npt, i - 1), c)),
                   pl.BlockSpec((SUBLANES, tf), lambda c, i: (0, c)),
                   pl.BlockSpec((b, tf), lambda c, i: (0, c))],
        out_shape=[jax.ShapeDtypeStruct((m, f), BF16),
                   jax.ShapeDtypeStruct((SUBLANES, f), F32),
                   jax.ShapeDtypeStruct((b, f), F32)],
        scratch_shapes=[pltpu.VMEM((d, tf), BF16), pltpu.VMEM((d, tf), BF16),
                        pltpu.VMEM((SUBLANES, tf), F32)],
        compiler_params=_cparams(("arbitrary", "arbitrary"), 60),
        name="ffn",
    )(h2, h2, wg, wu, cw, state[:, 0], state[:, 1])


def _rope_tables(t, b):
    inv = ROPE_THETA ** (-jnp.arange(0, ROT_DIM, 2, dtype=F32) / ROT_DIM)
    pos = jnp.concatenate([jnp.arange(t, dtype=jnp.int32),
                           jnp.full((b,), PAST_LEN, jnp.int32)]).astype(F32)
    ang = pos[:, None] * inv[None, :]
    cos, sin = jnp.cos(ang), jnp.sin(ang)
    m = t + b
    pad = LANES - ROT_DIM
    cos_t = jnp.concatenate([cos, cos, jnp.ones((m, pad), F32)], axis=1)
    sin_t = jnp.concatenate([-sin, sin, jnp.zeros((m, pad), F32)], axis=1)
    return cos_t, sin_t


def kernel(x_prompt, x_sample, cache_kv_w128, cache_kv_w512, cache_kv_w2048, state_conv_qkv, state_delta,
           state_ffn_conv, norm_mix, w_in, conv_qkv, a_log, dt_bias, delta_norm, w_a_out, w_b_out, w_out,
           norm_ffn, w_gate, ffn_conv, w_up, w_down, norm_final):
    assert norm_mix.shape[0] == 1, "single layer only"
    assert x_prompt.shape[0] == 1 and x_sample.shape[1] == 1
    t, d = x_prompt.shape[1], x_prompt.shape[2]
    b = x_sample.shape[0]
    assert t % SUPER == 0 and b == ROW_BLOCK
    caches = (cache_kv_w128[0], cache_kv_w512[0], cache_kv_w2048[0])
    for (win, _), c in zip(DIL_GROUPS, caches):
        assert c.shape[1] == win, "cache must hold a full window"

    off_bqkv = A_COLS
    off_bz = off_bqkv + B_CONV_COLS
    off_ba = off_bz + B_WIDTH
    off_gate = off_ba + 2 * B_HEADS
    assert w_in.shape[2] == off_gate + 2 * d
    w_in_t = jnp.swapaxes(w_in[0], 0, 1)
    w_main = _bf(w_in_t)
    w_ba = jnp.pad(w_in_t[off_ba:off_gate], ((0, LANES - 2 * B_HEADS), (0, 0)))
    w_ba_hi = _bf(w_ba)
    w_ba = jnp.concatenate([w_ba_hi, _bf(w_ba - w_ba_hi.astype(F32))], axis=0)
    gate_col0 = off_ba

    xp = x_prompt.reshape(t, d)
    xs = x_sample.reshape(b, d)
    h, ba = _norm_in(xp, xs, norm_mix, w_ba)
    cos_t, sin_t = _rope_tables(t, b)
    proj = _proj(h, w_main, cos_t, sin_t, gate_col0, off_gate)

    oa_p = _attn_prompt(proj, t)
    oa_s = _attn_sample(proj, caches, t, b)

    alog_row = jnp.pad(a_log[0], (B_HEADS, LANES - 2 * B_HEADS)).reshape(1, LANES)
    dtb_row = jnp.pad(dt_bias[0], (B_HEADS, LANES - 2 * B_HEADS)).reshape(1, LANES)
    ltri = jnp.tril(jnp.ones((CHUNK, CHUNK), F32))
    head_of_col = jnp.arange(2 * B_WIDTH, dtype=jnp.int32) // LANES
    expand = (jnp.arange(LANES, dtype=jnp.int32)[:, None] == head_of_col[None, :]).astype(F32)
    cw_qkv = conv_qkv[0]
    ob_p, delta_p = _delta_prompt(proj, ba, alog_row, dtb_row, ltri, cw_qkv, delta_norm, t, off_bqkv, off_bz)
    beta_s, g_s = _beta_decay(ba, alog_row, dtb_row, ltri, expand, t, b, False)
    qkv_s = _conv_sample(proj, state_conv_qkv[0], cw_qkv, t, b, off_bqkv)
    ob_s, delta_s = _delta_sample(qkv_s, beta_s, g_s, proj, delta_norm, state_delta[0], t, b, off_bz)

    merged = _merge(oa_p, oa_s, ob_p, ob_s, _bf(w_a_out[0]), _bf(w_b_out[0]), proj, gate_col0)
    x1 = _out_proj(merged, _bf(w_out[0]), xp, xs)

    h2 = _norm_mid(x1, norm_ffn)
    act, g_tail, g_smp = _ffn(h2, w_gate[0], w_up[0], ffn_conv[0], state_ffn_conv[0], t, b)
    x2 = _mm_res(act, _bf(w_down[0]), x1, 640, 512, "down_proj")
    y_p, y_s = _norm_out(x2, norm_final.reshape(1, d), t, b)

    kv_p, kv_s = [], []
    for g, (win, _) in enumerate(DIL_GROUPS):
        c0 = g * 3 * A_WIDTH + A_WIDTH
        keep = min(win, t)
        kv_p.append(proj[t - keep:t, c0:c0 + 2 * A_WIDTH].reshape(1, 1, keep, 2, A_HEADS, HEAD_DIM))
        kv_s.append(proj[t:, c0:c0 + 2 * A_WIDTH].reshape(1, b, 1, 2, A_HEADS, HEAD_DIM))
    conv_p = proj[t - (B_CONV - 1):t, off_bqkv:off_bz].reshape(1, 1, B_CONV - 1, B_CONV_COLS)
    conv_s = jnp.concatenate([state_conv_qkv[0][:, 1:], proj[t:, off_bqkv:off_bz][:, None]], axis=1)[None]
    ffn_p = g_tail[SUBLANES - (FFN_CONV - 1):].reshape(1, 1, FFN_CONV - 1, -1)
    ffn_s = jnp.concatenate([state_ffn_conv[0][:, 1:], g_smp[:, None]], axis=1)[None]
    return (y_p.reshape(1, t, d), y_s.reshape(b, 1, d), kv_p[0], kv_p[1], kv_p[2],
            conv_p, delta_p.reshape(1, 1, B_HEADS, B_DK, B_DV), ffn_p,
            kv_s[0], kv_s[1], kv_s[2], conv_s, delta_s[None], ffn_s)
```

```python
import functools

import jax
import jax.numpy as jnp
from jax import lax
from jax.experimental import pallas as pl
from jax.experimental.pallas import tpu as pltpu

F32 = jnp.float32
BF16 = jnp.bfloat16
HIGHEST = lax.Precision.HIGHEST

HEAD_DIM = 128
ROT_DIM = HEAD_DIM // 4
ROT_HALF = ROT_DIM // 2
ROPE_THETA = 500000.0
DIL_GROUPS = ((128, 1), (512, 4), (2048, 16))
N_GROUPS = len(DIL_GROUPS)
A_HEADS = 4
A_WIDTH = A_HEADS * HEAD_DIM
A_COLS = N_GROUPS * 3 * A_WIDTH
KEYS_PER_QUERY = 128
B_HEADS = 16
B_DK = 128
B_DV = 128
B_WIDTH = B_HEADS * B_DK
B_CONV = 4
B_CONV_COLS = 3 * B_WIDTH
FFN_CONV = 3
EPS = 1e-6
PAST_LEN = 2048

LANES = 128
SUBLANES = 8
ROW_BLOCK = 128
SUPER = 2048
CHUNK = 128
HEAD_GROUP = 4
COL_BLOCK = HEAD_GROUP * LANES
VMEM_CAP_MB = 60
MAX_ROW_PARTS = 5
ROW_TILE_CAP = 2080
BF16_ROWS = 16


def _row_parts(tm):
    n = max(p for p in range(1, MAX_ROW_PARTS + 1) if tm % (p * BF16_ROWS) == 0)
    return [slice(p * (tm // n), (p + 1) * (tm // n)) for p in range(n)]


def _cparams(sem, vmem_mb):
    return pltpu.CompilerParams(dimension_semantics=sem,
                                vmem_limit_bytes=int(min(vmem_mb, VMEM_CAP_MB) * 2 ** 20))


def _bf(x):
    return x.astype(BF16)


def _dot(a, b):
    return jnp.dot(a, b, preferred_element_type=F32)


def _dot_nt(a, b):
    return lax.dot_general(a, b, (((1,), (1,)), ((), ())), preferred_element_type=F32)


def _dot_tn(a, b):
    return lax.dot_general(a, b, (((0,), (0,)), ((), ())), preferred_element_type=F32)


def _silu(x):
    return x * jax.nn.sigmoid(x)


NORM_ROWS = 256


def _norm_in_kernel(xp_ref, xs_ref, nw_ref, wba_ref, h_ref, ba_ref, *, n_prompt_blocks, bs):
    i = pl.program_id(0)

    def body(x, rows):
        y = x * lax.rsqrt(jnp.mean(x * x, axis=-1, keepdims=True) + EPS) * nw_ref[...]
        y_hi = _bf(y)
        y_lo = _bf(y - y_hi.astype(F32))
        h_ref[rows, :] = y_hi
        hi = _dot_nt(y_hi, wba_ref[...])
        ba_ref[rows, :] = hi[:, :LANES] + hi[:, LANES:] + _dot_nt(y_lo, wba_ref[0:LANES, :])

    @pl.when(i < n_prompt_blocks)
    def _():
        body(xp_ref[...], slice(None))

    @pl.when(i == n_prompt_blocks)
    def _():
        body(xs_ref[...], slice(0, bs))


def _norm_in(xp, xs, nw, wba):
    t, d = xp.shape
    b = xs.shape[0]
    rb = NORM_ROWS
    npb = t // rb
    m = t + b
    return pl.pallas_call(
        functools.partial(_norm_in_kernel, n_prompt_blocks=npb, bs=b),
        grid=(npb + 1,),
        in_specs=[
            pl.BlockSpec((rb, d), lambda i: (jnp.minimum(i, npb - 1), 0)),
            pl.BlockSpec((b, d), lambda i: (0, 0)),
            pl.BlockSpec((1, d), lambda i: (0, 0)),
            pl.BlockSpec((2 * LANES, d), lambda i: (0, 0)),
        ],
        out_specs=[
            pl.BlockSpec((rb, d), lambda i: (i, 0)),
            pl.BlockSpec((rb, LANES), lambda i: (i, 0)),
        ],
        out_shape=[
            jax.ShapeDtypeStruct((m, d), BF16),
            jax.ShapeDtypeStruct((m, LANES), F32),
        ],
        compiler_params=_cparams(("arbitrary",), 40),
        name="norm_in",
    )(xp, xs, nw, wba)


def _norm_mid_kernel(x_ref, nw_ref, h_ref):
    x = x_ref[...]
    y = x * lax.rsqrt(jnp.mean(x * x, axis=-1, keepdims=True) + EPS) * nw_ref[...]
    h_ref[...] = _bf(y)


def _norm_mid(x, nw):
    m, d = x.shape
    rb = _row_tile(m, 640)
    return pl.pallas_call(
        _norm_mid_kernel,
        grid=(m // rb,),
        in_specs=[pl.BlockSpec((rb, d), lambda i: (i, 0)),
                  pl.BlockSpec((1, d), lambda i: (0, 0))],
        out_specs=pl.BlockSpec((rb, d), lambda i: (i, 0)),
        out_shape=jax.ShapeDtypeStruct((m, d), BF16),
        compiler_params=_cparams(("arbitrary",), 40),
        name="norm_mid",
    )(x, nw)


def _norm_out_kernel(x_ref, nw_ref, yp_ref, ys_ref, *, n_prompt_blocks, bs):
    i = pl.program_id(0)

    def norm(x):
        return x * lax.rsqrt(jnp.mean(x * x, axis=-1, keepdims=True) + EPS) * nw_ref[...]

    @pl.when(i < n_prompt_blocks)
    def _():
        yp_ref[...] = norm(x_ref[...])

    @pl.when(i == n_prompt_blocks)
    def _():
        ys_ref[...] = norm(x_ref[0:bs, :])


def _norm_out(x, nw, t, b):
    d = x.shape[1]
    rb = NORM_ROWS
    npb = t // rb
    return pl.pallas_call(
        functools.partial(_norm_out_kernel, n_prompt_blocks=npb, bs=b),
        grid=(npb + 1,),
        in_specs=[pl.BlockSpec((rb, d), lambda i: (i, 0)),
                  pl.BlockSpec((1, d), lambda i: (0, 0))],
        out_specs=[
            pl.BlockSpec((rb, d), lambda i: (jnp.minimum(i, npb - 1), 0)),
            pl.BlockSpec((b, d), lambda i: (0, 0)),
        ],
        out_shape=[jax.ShapeDtypeStruct((t, d), F32), jax.ShapeDtypeStruct((b, d), F32)],
        compiler_params=_cparams(("arbitrary",), 32),
        name="norm_out",
    )(x, nw)


def _row_tile(m, cap):
    best = 16
    for c in range(16, cap + 1, 16):
        if m % c == 0:
            best = c
    return best


def _proj_kernel(h_ref, w_ref, cos_ref, sin_ref, o_ref, *, n_attn_tiles, gate_tile0):
    j = pl.program_id(1)
    is_rope = jnp.logical_and(j < n_attn_tiles, j % 3 != 2)
    is_gate = j >= gate_tile0

    parts = _row_parts(h_ref.shape[0])

    def products():
        return [_dot_nt(h_ref[sl, :], w_ref[...]) for sl in parts]

    @pl.when(is_rope)
    def _():
        for sl, acc in zip(parts, products()):
            cos = cos_ref[sl, :]
            sin = sin_ref[sl, :]
            lane = lax.broadcasted_iota(jnp.int32, cos.shape, 1)
            for hh in range(A_HEADS):
                a = acc[:, hh * HEAD_DIM:(hh + 1) * HEAD_DIM]
                partner = jnp.where(lane < ROT_HALF,
                                    pltpu.roll(a, HEAD_DIM - ROT_HALF, 1),
                                    pltpu.roll(a, ROT_HALF, 1))
                o_ref[sl, hh * HEAD_DIM:(hh + 1) * HEAD_DIM] = a * cos + partner * sin

    @pl.when(is_gate)
    def _():
        for sl, acc in zip(parts, products()):
            o_ref[sl, :] = jax.nn.sigmoid(acc)

    @pl.when(jnp.logical_not(jnp.logical_or(is_rope, is_gate)))
    def _():
        for sl, acc in zip(parts, products()):
            o_ref[sl, :] = acc


def _proj(h, w, cos, sin, gate_col0, gate_row0):
    m, k = h.shape
    skip = gate_row0 - gate_col0
    n = w.shape[0] - skip
    tn = A_WIDTH
    tm = _row_tile(m, ROW_TILE_CAP)
    gate_tile0 = gate_col0 // tn
    return pl.pallas_call(
        functools.partial(_proj_kernel, n_attn_tiles=A_COLS // tn, gate_tile0=gate_tile0),
        grid=(m // tm, n // tn),
        in_specs=[
            pl.BlockSpec((tm, k), lambda i, j: (i, 0)),
            pl.BlockSpec((pl.Element(tn), pl.Element(k)),
                         lambda i, j: (pl.multiple_of(j * tn + jnp.where(j >= gate_tile0, skip, 0),
                                                      2 * SUBLANES), 0)),
            pl.BlockSpec((tm, LANES), lambda i, j: (i, 0)),
            pl.BlockSpec((tm, LANES), lambda i, j: (i, 0)),
        ],
        out_specs=pl.BlockSpec((tm, tn), lambda i, j: (i, j)),
        out_shape=jax.ShapeDtypeStruct((m, n), F32),
        compiler_params=_cparams(("arbitrary", "arbitrary"), 60),
        name="proj_in",
    )(h, w, cos, sin)


def _merge_kernel(oap_ref, oas_ref, obp_ref, obs_ref, wa_ref, wb_ref, ga_ref, gb_ref, o_ref,
                  *, n_prompt_tiles, bs):
    i = pl.program_id(0)

    def body(oa, ob, rows):
        a = _dot(oa, wa_ref[...])
        b = _dot(ob, wb_ref[...])
        o_ref[rows, :] = _bf(ga_ref[rows, :] * a + gb_ref[rows, :] * b)

    @pl.when(i < n_prompt_tiles)
    def _():
        body(oap_ref[...], obp_ref[...], slice(None))

    @pl.when(i == n_prompt_tiles)
    def _():
        body(_bf(oas_ref[...]), _bf(obs_ref[...]), slice(0, bs))


def _merge(oa_p, oa_s, ob_p, ob_s, wa, wb, proj, gate_col0):
    t, b = oa_p.shape[0], oa_s.shape[0]
    n = wa.shape[1]
    tn = COL_BLOCK
    tm = _row_tile(t, 2048)
    npt = t // tm
    ga0 = gate_col0 // tn
    gb0 = (gate_col0 + n) // tn
    prompt = lambda width: pl.BlockSpec((tm, width), lambda i, j: (jnp.minimum(i, npt - 1), 0))
    sample = lambda width: pl.BlockSpec((b, width), lambda i, j: (0, 0))
    return pl.pallas_call(
        functools.partial(_merge_kernel, n_prompt_tiles=npt, bs=b),
        grid=(npt + 1, n // tn),
        in_specs=[
            prompt(oa_p.shape[1]), sample(oa_s.shape[1]), prompt(ob_p.shape[1]), sample(ob_s.shape[1]),
            pl.BlockSpec((wa.shape[0], tn), lambda i, j: (0, j)),
            pl.BlockSpec((wb.shape[0], tn), lambda i, j: (0, j)),
            pl.BlockSpec((tm, tn), lambda i, j: (i, ga0 + j)),
            pl.BlockSpec((tm, tn), lambda i, j: (i, gb0 + j)),
        ],
        out_specs=pl.BlockSpec((tm, tn), lambda i, j: (i, j)),
        out_shape=jax.ShapeDtypeStruct((t + b, n), BF16),
        compiler_params=_cparams(("arbitrary", "arbitrary"), 58),
        name="merge",
    )(oa_p, oa_s, ob_p, ob_s, wa, wb, proj, proj)


def _out_proj_kernel(a_ref, w_ref, rp_ref, rs_ref, wn_ref, o_ref, wn_bf_ref, *, n_prompt_tiles, bs):
    i = pl.program_id(0)

    @pl.when(i < n_prompt_tiles)
    def _():
        o_ref[...] = rp_ref[...] + _dot(a_ref[...], w_ref[...])
        wn_bf_ref[...] = _bf(wn_ref[...])

    @pl.when(i == n_prompt_tiles)
    def _():
        o_ref[0:bs, :] = rs_ref[...] + _dot(a_ref[0:bs, :], w_ref[...])
        wn_bf_ref[...] = _bf(wn_ref[...])


def _out_proj(a, w, res_p, res_s, w_next):
    m, k = a.shape
    t, b = res_p.shape[0], res_s.shape[0]
    n = w.shape[1]
    tn = COL_BLOCK
    tm = _row_tile(t, 1024)
    npt = t // tm
    nj = n // tn
    steps = (npt + 1) * nj
    kn, nn = w_next.shape
    cast_rows = pl.cdiv(pl.cdiv(kn, steps), BF16_ROWS) * BF16_ROWS
    last_block = pl.cdiv(kn, cast_rows) - 1
    cast_spec = pl.BlockSpec((cast_rows, nn), lambda i, j: (jnp.minimum(i * nj + j, last_block), 0))
    return pl.pallas_call(
        functools.partial(_out_proj_kernel, n_prompt_tiles=npt, bs=b),
        grid=(npt + 1, nj),
        in_specs=[
            pl.BlockSpec((tm, k), lambda i, j: (i, 0)),
            pl.BlockSpec((k, tn), lambda i, j: (0, j)),
            pl.BlockSpec((tm, tn), lambda i, j: (jnp.minimum(i, npt - 1), j)),
            pl.BlockSpec((b, tn), lambda i, j: (0, j)),
            cast_spec,
        ],
        out_specs=[pl.BlockSpec((tm, tn), lambda i, j: (i, j)), cast_spec],
        out_shape=[jax.ShapeDtypeStruct((m, n), F32), jax.ShapeDtypeStruct((kn, nn), BF16)],
        compiler_params=_cparams(("arbitrary", "arbitrary"), 48),
        name="out_proj",
    )(a, w, res_p, res_s, w_next)


def _mm_res_kernel(a_ref, w_ref, r_ref, o_ref):
    o_ref[...] = r_ref[...] + _dot(a_ref[...], w_ref[...])


def _mm_res(a, w, res, tm_cap, tn, name):
    m, k = a.shape
    n = w.shape[1]
    tm = _row_tile(m, tm_cap)
    return pl.pallas_call(
        _mm_res_kernel,
        grid=(m // tm, n // tn),
        in_specs=[
            pl.BlockSpec((tm, k), lambda i, j: (i, 0)),
            pl.BlockSpec((k, tn), lambda i, j: (0, j)),
            pl.BlockSpec((tm, tn), lambda i, j: (i, j)),
        ],
        out_specs=pl.BlockSpec((tm, tn), lambda i, j: (i, j)),
        out_shape=jax.ShapeDtypeStruct((m, n), F32),
        compiler_params=_cparams(("arbitrary", "arbitrary"), 60),
        name=name,
    )(a, w, res)


ATTN_TILE_BATCH = 8


def _attn_prompt_kernel(*refs):
    grp = [refs[5 * g:5 * g + 5] for g in range(N_GROUPS)]
    o_ref, o_scr, l_scr = refs[5 * N_GROUPS:]
    first = pl.program_id(1) == 0
    row = lax.broadcasted_iota(jnp.int32, (LANES, LANES), 0)
    col = lax.broadcasted_iota(jnp.int32, (LANES, LANES), 1)
    mask_cur = col <= row
    mask_prev = col >= row
    mask_prev_first = col >= row + jnp.where(first, LANES, 0)
    scale = HEAD_DIM ** -0.5
    nq = KEYS_PER_QUERY

    def rows(dil, base, r):
        if dil == 1:
            return pl.ds(base, nq)
        return pl.ds(base + r, nq, stride=dil)

    tiles = [(g, dil, sb, r) for g, (_, dil) in enumerate(DIL_GROUPS)
             for sb in range(SUPER // (nq * dil)) for r in range(dil)]
    for t0 in range(0, len(tiles), ATTN_TILE_BATCH):
        batch = tiles[t0:t0 + ATTN_TILE_BATCH]
        scores = []
        for g, dil, sb, r in batch:
            q_ref, k_ref, v_ref, kp_ref, vp_ref = grp[g]
            span = nq * dil
            cur = rows(dil, sb * span, r)
            qt = _bf(q_ref[cur, :])
            if sb > 0:
                prev = rows(dil, (sb - 1) * span, r)
                kp, vp, mp = k_ref[prev, :], v_ref[prev, :], mask_prev
            else:
                prev = rows(dil, 0, r)
                kp, vp, mp = kp_ref[prev, :], vp_ref[prev, :], mask_prev_first
            s_c = jnp.where(mask_cur, _dot_nt(qt, _bf(k_ref[cur, :])) * scale, -jnp.inf)
            s_p = jnp.where(mp, _dot_nt(qt, _bf(kp)) * scale, -jnp.inf)
            scores.append((s_c, s_p, _bf(v_ref[cur, :]), _bf(vp)))
        probs = []
        for s_c, s_p, vc, vp in scores:
            mx = jnp.maximum(jnp.max(s_c, axis=-1, keepdims=True),
                             jnp.max(s_p, axis=-1, keepdims=True))
            p_c = jnp.exp(s_c - mx)
            p_p = jnp.exp(s_p - mx)
            den = jnp.sum(p_c, axis=-1, keepdims=True) + jnp.sum(p_p, axis=-1, keepdims=True)
            probs.append((_bf(p_c), _bf(p_p), vc, vp, den, mx))
        for (g, dil, sb, r), (p_c, p_p, vc, vp, den, mx) in zip(batch, probs):
            o = (_dot(p_c, vc) + _dot(p_p, vp)) / den
            dst = rows(dil, sb * nq * dil, r)
            o_scr[g, dst, :] = o
            l_scr[g, dst, :] = jnp.broadcast_to(mx + jnp.log(den), (nq, LANES))

    l0, l1, l2 = l_scr[0], l_scr[1], l_scr[2]
    mx = jnp.maximum(jnp.maximum(l0, l1), l2)
    e0, e1, e2 = jnp.exp(l0 - mx), jnp.exp(l1 - mx), jnp.exp(l2 - mx)
    o = (e0 * o_scr[0] + e1 * o_scr[1] + e2 * o_scr[2]) / (e0 + e1 + e2)
    o_ref[...] = _bf(o)


def _attn_prompt(proj, t):
    nsup = t // SUPER
    in_specs = []
    args = []
    for g, (_, dil) in enumerate(DIL_GROUPS):
        span = KEYS_PER_QUERY * dil
        per = SUPER // span
        cb = g * 3 * A_HEADS
        for part in range(3):
            in_specs.append(pl.BlockSpec(
                (SUPER, HEAD_DIM), lambda h, i, c=cb + part * A_HEADS: (i, c + h)))
            args.append(proj)
        for part in (1, 2):
            in_specs.append(pl.BlockSpec(
                (span, HEAD_DIM),
                lambda h, i, c=cb + part * A_HEADS, per=per: (jnp.maximum(i * per - 1, 0), c + h)))
            args.append(proj)
    return pl.pallas_call(
        _attn_prompt_kernel,
        grid=(A_HEADS, nsup),
        in_specs=in_specs,
        out_specs=pl.BlockSpec((SUPER, HEAD_DIM), lambda h, i: (i, h)),
        out_shape=jax.ShapeDtypeStruct((t, A_WIDTH), BF16),
        scratch_shapes=[pltpu.VMEM((N_GROUPS, SUPER, HEAD_DIM), F32),
                        pltpu.VMEM((N_GROUPS, SUPER, HEAD_DIM), F32)],
        compiler_params=_cparams(("arbitrary", "arbitrary"), 48),
        name="attn_prompt",
    )(*args)


def _attn_sample_kernel(qkv_ref, c0_ref, c1_ref, c2_ref, o_ref, *, bb):
    caches = (c0_ref, c1_ref, c2_ref)
    scale = HEAD_DIM ** -0.5
    sub = lax.broadcasted_iota(jnp.int32, (2 * A_HEADS, HEAD_DIM), 0)

    def on_sublanes(b, col0, first):
        out = jnp.zeros((2 * A_HEADS, HEAD_DIM), F32)
        for hh in range(A_HEADS):
            r = qkv_ref[b:b + 1, col0 + hh * HEAD_DIM:col0 + (hh + 1) * HEAD_DIM]
            out = jnp.where(sub == first + hh, r, out)
        return out

    for b in range(bb):
        outs, lses = [], []
        for g in range(N_GROUPS):
            c0 = g * 3 * A_WIDTH
            q8 = on_sublanes(b, c0, 0)
            kn8 = on_sublanes(b, c0 + A_WIDTH, 0)
            vn8 = on_sublanes(b, c0 + 2 * A_WIDTH, A_HEADS)
            kv = caches[g][b]
            s = jnp.sum(kv * q8, axis=-1, keepdims=True) * scale
            s = pltpu.roll(jnp.broadcast_to(s, kv.shape), A_HEADS, 1)
            sn = jnp.sum(kn8 * q8, axis=-1, keepdims=True) * scale
            sn = pltpu.roll(jnp.broadcast_to(sn, q8.shape), A_HEADS, 0)
            mx = jnp.maximum(jnp.max(s, axis=0), sn)
            p = jnp.exp(s - mx)
            pn = jnp.exp(sn - mx)
            den = jnp.sum(p, axis=0) + pn
            outs.append((jnp.sum(p * kv, axis=0) + pn * vn8) / den)
            lses.append(mx + jnp.log(den))
        mx = jnp.maximum(jnp.maximum(lses[0], lses[1]), lses[2])
        es = [jnp.exp(l - mx) for l in lses]
        o = (es[0] * outs[0] + es[1] * outs[1] + es[2] * outs[2]) / (es[0] + es[1] + es[2])
        for hh in range(A_HEADS):
            o_ref[b:b + 1, hh * HEAD_DIM:(hh + 1) * HEAD_DIM] = o[A_HEADS + hh:A_HEADS + hh + 1, :]


def _attn_sample(proj, caches, t, b):
    bb = SUBLANES
    in_specs = [pl.BlockSpec((bb, A_COLS), lambda i: (t // bb + i, 0))]
    args = [proj]
    for (win, dil), c in zip(DIL_GROUPS, caches):
        args.append(c.reshape(b, win // dil, dil, 2 * A_HEADS, HEAD_DIM))
        in_specs.append(pl.BlockSpec((bb, win // dil, None, 2 * A_HEADS, HEAD_DIM),
                                     lambda i: (i, 0, 0, 0, 0)))
    return pl.pallas_call(
        functools.partial(_attn_sample_kernel, bb=bb),
        grid=(b // bb,),
        in_specs=in_specs,
        out_specs=pl.BlockSpec((bb, A_WIDTH), lambda i: (i, 0)),
        out_shape=jax.ShapeDtypeStruct((b, A_WIDTH), F32),
        compiler_params=_cparams(("arbitrary",), 40),
        name="attn_sample",
    )(*args)


def _beta_decay_kernel(ba_ref, alog_ref, dtb_ref, ltri_ref, e_ref, beta_ref, gc_ref, *, cumulative):
    x = ba_ref[...]
    lane = lax.broadcasted_iota(jnp.int32, x.shape, 1)
    z = x + dtb_ref[...]
    softplus = jnp.maximum(z, 0.0) + jnp.log1p(jnp.exp(-jnp.abs(z)))
    g = -jnp.exp(alog_ref[...]) * softplus
    if cumulative:
        g = jnp.dot(ltri_ref[...], g, precision=HIGHEST, preferred_element_type=F32)
    vals = jnp.where(lane < B_HEADS, jax.nn.sigmoid(x), g)
    wide = jnp.dot(vals, e_ref[...], precision=HIGHEST, preferred_element_type=F32)
    beta_ref[...] = wide[:, :B_WIDTH]
    gc_ref[...] = wide[:, B_WIDTH:]


def _beta_decay(ba, alog_row, dtb_row, ltri, expand, row0, rows, cumulative):
    blk0 = row0 // CHUNK
    return pl.pallas_call(
        functools.partial(_beta_decay_kernel, cumulative=cumulative),
        grid=(rows // CHUNK,),
        in_specs=[
            pl.BlockSpec((CHUNK, LANES), lambda i: (blk0 + i, 0)),
            pl.BlockSpec((1, LANES), lambda i: (0, 0)),
            pl.BlockSpec((1, LANES), lambda i: (0, 0)),
            pl.BlockSpec((CHUNK, CHUNK), lambda i: (0, 0)),
            pl.BlockSpec((LANES, 2 * B_WIDTH), lambda i: (0, 0)),
        ],
        out_specs=[pl.BlockSpec((CHUNK, B_WIDTH), lambda i: (i, 0)),
                   pl.BlockSpec((CHUNK, B_WIDTH), lambda i: (i, 0))],
        out_shape=[jax.ShapeDtypeStruct((rows, B_WIDTH), F32),
                   jax.ShapeDtypeStruct((rows, B_WIDTH), F32)],
        compiler_params=_cparams(("arbitrary",), 32),
        name="beta_decay",
    )(ba, alog_row, dtb_row, ltri, expand)


def _l2norm_heads(x):
    parts = []
    for hh in range(x.shape[1] // HEAD_DIM):
        a = x[:, hh * HEAD_DIM:(hh + 1) * HEAD_DIM]
        parts.append(a * lax.rsqrt(jnp.sum(a * a, axis=-1, keepdims=True) + EPS))
    return jnp.concatenate(parts, axis=1)


def _gated_norm(o, z, nw):
    y = o * lax.rsqrt(jnp.mean(o * o, axis=-1, keepdims=True) + EPS) * nw
    return y * _silu(z)


INV_BASE = 16


assert CHUNK == LANES


def _unit_lower_inverse(a, eye, row, col):
    c = a[0].shape[0]

    def blk(size):
        sh = size.bit_length() - 1
        return (row >> sh) == (col >> sh)

    base = blk(INV_BASE)
    d = [jnp.where(base, x, 0.0) for x in a]
    tm = [eye - x for x in d]
    pw = [_bf(x) for x in d]
    for _ in range(INV_BASE.bit_length() - 2):
        pw = [_bf(_dot(x, x)) for x in pw]
        tm = [x + _dot(_bf(x), p) for x, p in zip(tm, pw)]
    size = INV_BASE
    while size < c:
        sel = jnp.logical_and(blk(2 * size), jnp.logical_not(blk(size)))
        tm_bf = [_bf(x) for x in tm]
        inner = [_bf(_dot(_bf(jnp.where(sel, x, 0.0)), y)) for x, y in zip(a, tm_bf)]
        tm = [x - _dot(y, z) for x, y, z in zip(tm, tm_bf, inner)]
        size *= 2
    return tm


def _delta_prompt_kernel(x_ref, z_ref, ba_ref, alog_ref, dtb_ref, ltri_ref, cw_ref, nw_ref,
                         o_ref, sfin_ref, s_scr, buf):
    j = pl.program_id(0)
    c = CHUNK
    halo = SUBLANES
    heads = range(B_HEADS)

    @pl.when(j == 0)
    def _():
        s_scr[...] = jnp.zeros_like(s_scr)
        buf[0:halo, :] = jnp.zeros((halo, buf.shape[1]), F32)

    row = lax.broadcasted_iota(jnp.int32, (c, c), 0)
    col = lax.broadcasted_iota(jnp.int32, (c, c), 1)
    tril = col <= row
    strict = col < row
    eye = (col == row).astype(F32)

    logits = ba_ref[...]
    zz = logits + dtb_ref[...]
    softplus = jnp.maximum(zz, 0.0) + jnp.log1p(jnp.exp(-jnp.abs(zz)))
    g_cum = jnp.dot(ltri_ref[...], -jnp.exp(alog_ref[...]) * softplus,
                    precision=HIGHEST, preferred_element_type=F32)
    lane = lax.broadcasted_iota(jnp.int32, logits.shape, 1)
    per_head = jnp.where(lane < B_HEADS, jax.nn.sigmoid(logits), g_cum).T
    gc_t = [jnp.broadcast_to(per_head[B_HEADS + hh:B_HEADS + hh + 1, :], (c, c)) for hh in heads]
    gc = [x.T for x in gc_t]
    beta = [jnp.broadcast_to(per_head[hh:hh + 1, :], (c, c)).T for hh in heads]

    buf[halo:halo + c, :] = x_ref[...]

    def conv(c0):
        cs = slice(c0, c0 + LANES)
        acc = buf[pl.ds(halo, c), cs] * cw_ref[B_CONV - 1:B_CONV, cs]
        for s in range(1, B_CONV):
            acc = acc + buf[pl.ds(halo - s, c), cs] * cw_ref[B_CONV - 1 - s:B_CONV - s, cs]
        return _silu(acc)

    def l2n(a):
        return a * lax.rsqrt(jnp.sum(a * a, axis=-1, keepdims=True) + EPS)

    q = [l2n(conv(hh * LANES)) * (B_DK ** -0.5) for hh in heads]
    k = [l2n(conv(B_WIDTH + hh * LANES)) for hh in heads]
    v = [conv(2 * B_WIDTH + hh * LANES) for hh in heads]
    buf[0:halo, :] = buf[c:c + halo, :]

    k_bf = [_bf(x) for x in k]
    kb = [x * bb for x, bb in zip(k, beta)]
    kq = [_dot_nt(jnp.concatenate([_bf(x), _bf(y)], axis=0), kk) for x, y, kk in zip(kb, q, k_bf)]
    decay = [jnp.exp(jnp.where(tril, g - gt, -jnp.inf)) for g, gt in zip(gc, gc_t)]
    a = [jnp.where(strict, x[:c] * dd, 0.0) for x, dd in zip(kq, decay)]
    qk_bf = [_bf(x[c:] * dd) for x, dd in zip(kq, decay)]
    tm = _unit_lower_inverse(a, eye, row, col)
    e_g = [jnp.exp(g) for g in gc]
    uw = [_dot(_bf(t), jnp.concatenate([_bf(vv * bb), _bf(x * eg)], axis=1))
          for t, vv, bb, x, eg in zip(tm, v, beta, kb, e_g)]
    s = [s_scr[hh] for hh in heads]
    s_bf = [_bf(x) for x in s]
    ws = [_dot(jnp.concatenate([_bf(x[:, LANES:]), _bf(qq * eg)], axis=0), sb)
          for x, qq, eg, sb in zip(uw, q, e_g, s_bf)]
    v_bf = [_bf(x[:, :LANES] - y[:c]) for x, y in zip(uw, ws)]
    o = [y[c:] + _dot(qq, vv) for y, qq, vv in zip(ws, qk_bf, v_bf)]
    g_last = [g[c - 1:c, :] for g in gc]
    kd_bf = [_bf(kk * jnp.exp(gl - g)) for kk, gl, g in zip(k, g_last, gc)]
    for hh in heads:
        sl = slice(hh * LANES, (hh + 1) * LANES)
        s_scr[hh] = s[hh] * jnp.exp(g_last[hh]) + _dot_tn(kd_bf[hh], v_bf[hh])
        o_ref[:, sl] = _gated_norm(o[hh], z_ref[:, sl], nw_ref[...]).astype(BF16)

    @pl.when(j == pl.num_programs(0) - 1)
    def _():
        sfin_ref[...] = s_scr[...]


def _delta_prompt(proj, ba, alog_row, dtb_row, ltri, cw, nw, t, x_col0, z_col0):
    c = CHUNK
    const = lambda shape: pl.BlockSpec(shape, lambda j: (0,) * len(shape))
    return pl.pallas_call(
        _delta_prompt_kernel,
        grid=(t // c,),
        in_specs=[
            pl.BlockSpec((pl.Element(c), pl.Element(B_CONV_COLS)),
                         lambda j: (pl.multiple_of(j * c, c), x_col0)),
            pl.BlockSpec((pl.Element(c), pl.Element(B_WIDTH)),
                         lambda j: (pl.multiple_of(j * c, c), z_col0)),
            pl.BlockSpec((c, LANES), lambda j: (j, 0)),
            const((1, LANES)), const((1, LANES)), const((c, c)), const((B_CONV, B_CONV_COLS)),
            const((1, LANES)),
        ],
        out_specs=[pl.BlockSpec((c, B_WIDTH), lambda j: (j, 0)),
                   const((B_HEADS, B_DK, B_DV))],
        out_shape=[jax.ShapeDtypeStruct((t, B_WIDTH), BF16),
                   jax.ShapeDtypeStruct((B_HEADS, B_DK, B_DV), F32)],
        scratch_shapes=[pltpu.VMEM((B_HEADS, B_DK, B_DV), F32),
                        pltpu.VMEM((c + SUBLANES, B_CONV_COLS), F32)],
        compiler_params=_cparams(("arbitrary",), 40),
        name="delta_prompt",
    )(proj, proj, ba, alog_row, dtb_row, ltri, cw, nw)


def _conv_sample_kernel(x_ref, s0_ref, s1_ref, s2_ref, cw_ref, o_ref):
    c = pl.program_id(0)
    acc = (s0_ref[...] * cw_ref[0:1, :] + s1_ref[...] * cw_ref[1:2, :]
           + s2_ref[...] * cw_ref[2:3, :] + x_ref[...] * cw_ref[3:4, :])
    y = _silu(acc)
    n_qk = 2 * B_HEADS // HEAD_GROUP

    @pl.when(c < n_qk // 2)
    def _():
        o_ref[...] = _l2norm_heads(y) * (B_DK ** -0.5)

    @pl.when(jnp.logical_and(c >= n_qk // 2, c < n_qk))
    def _():
        o_ref[...] = _l2norm_heads(y)

    @pl.when(c >= n_qk)
    def _():
        o_ref[...] = y


def _conv_sample(proj, state, cw, t, b, col0):
    ncb = B_CONV_COLS // COL_BLOCK
    cb0 = col0 // COL_BLOCK
    rb0 = t // b
    sspec = pl.BlockSpec((b, COL_BLOCK), lambda c: (0, c))
    return pl.pallas_call(
        _conv_sample_kernel,
        grid=(ncb,),
        in_specs=[pl.BlockSpec((b, COL_BLOCK), lambda c: (rb0, cb0 + c)), sspec, sspec, sspec,
                  pl.BlockSpec((B_CONV, COL_BLOCK), lambda c: (0, c))],
        out_specs=pl.BlockSpec((b, COL_BLOCK), lambda c: (0, c)),
        out_shape=jax.ShapeDtypeStruct((b, B_CONV_COLS), F32),
        compiler_params=_cparams(("arbitrary",), 32),
        name="conv_sample",
    )(proj, state[:, 0], state[:, 1], state[:, 2], cw)


def _delta_sample_kernel(q_ref, k_ref, v_ref, beta_ref, g_ref, z_ref, nw_ref, s_ref, o_ref, snew_ref, *, bb):
    row = lax.broadcasted_iota(jnp.int32, (LANES, LANES), 0)
    col = lax.broadcasted_iota(jnp.int32, (LANES, LANES), 1)
    eye = col == row

    def column(r):
        return jnp.sum(jnp.where(eye, r, 0.0), axis=-1, keepdims=True)

    units = [(b, hh, slice(hh * LANES, (hh + 1) * LANES)) for b in range(bb) for hh in range(HEAD_GROUP)]
    rows = [(q_ref[b:b + 1, sl], k_ref[b:b + 1, sl]) for b, _, sl in units]
    cols = [(column(q), column(k)) for q, k in rows]
    qk = [jnp.sum(q * k, axis=-1, keepdims=True) for q, k in rows]
    reads = [(jnp.sum(s_ref[b, hh] * qc, axis=0, keepdims=True),
              jnp.sum(s_ref[b, hh] * kc, axis=0, keepdims=True))
             for (b, hh, _), (qc, kc) in zip(units, cols)]
    outs = []
    for (b, hh, sl), (_, k_col), (qs, ks), qk_u in zip(units, cols, reads, qk):
        e_g = jnp.exp(g_ref[b:b + 1, sl])
        v_new = beta_ref[b:b + 1, sl] * (v_ref[b:b + 1, sl] - e_g * ks)
        outs.append(e_g * qs + qk_u * v_new)
        snew_ref[b, hh] = s_ref[b, hh] * e_g + k_col * v_new
    for (b, _, sl), o in zip(units, outs):
        o_ref[b:b + 1, sl] = _gated_norm(o, z_ref[b:b + 1, sl], nw_ref[...])


def _delta_sample(qkv, beta, g, proj, nw, state, t, b, z_col0):
    bb = SUBLANES
    hgs = B_HEADS // HEAD_GROUP
    zb0 = z_col0 // COL_BLOCK
    rb0 = t // bb
    blk = lambda off: pl.BlockSpec((bb, COL_BLOCK), lambda i, h, off=off: (i, off + h))
    sspec = pl.BlockSpec((bb, HEAD_GROUP, B_DK, B_DV), lambda i, h: (i, h, 0, 0))
    return pl.pallas_call(
        functools.partial(_delta_sample_kernel, bb=bb),
        grid=(b // bb, hgs),
        in_specs=[blk(0), blk(hgs), blk(2 * hgs), blk(0), blk(0),
                  pl.BlockSpec((bb, COL_BLOCK), lambda i, h: (rb0 + i, zb0 + h)),
                  pl.BlockSpec((1, LANES), lambda i, h: (0, 0)), sspec],
        out_specs=[pl.BlockSpec((bb, COL_BLOCK), lambda i, h: (i, h)), sspec],
        out_shape=[jax.ShapeDtypeStruct((b, B_WIDTH), F32),
                   jax.ShapeDtypeStruct(state.shape, F32)],
        compiler_params=_cparams(("arbitrary", "arbitrary"), 32),
        name="delta_sample",
    )(qkv, qkv, qkv, beta, g, proj, nw, state)


def _ffn_kernel(h_ref, hs_ref, wg_ref, wu_ref, cw_ref, s0_ref, s1_ref, act_ref, tail_ref, gs_ref,
                wg_bf, wu_bf, buf, *, tm, bs):
    i = pl.program_id(1)
    halo = SUBLANES

    @pl.when(i == 0)
    def _():
        wg_bf[...] = _bf(wg_ref[...])
        wu_bf[...] = _bf(wu_ref[...])
        buf[...] = jnp.zeros(buf.shape, F32)
        h = hs_ref[...]
        g = _dot(h, wg_bf[...])
        gs_ref[...] = g
        acc = s0_ref[...] * cw_ref[0:1, :] + s1_ref[...] * cw_ref[1:2, :] + g * cw_ref[2:3, :]
        act_ref[0:bs, :] = _bf(_silu(acc) * _dot(h, wu_bf[...]))

    @pl.when(i > 0)
    def _():
        parts = _row_parts(tm)
        rows = parts[0].stop
        gu = [(_dot(h_ref[sl, :], wg_bf[...]), _dot(h_ref[sl, :], wu_bf[...])) for sl in parts]
        prev = buf[...]
        first_rows = lax.broadcasted_iota(jnp.int32, prev.shape, 0)
        for sl, (g, u) in zip(parts, gu):
            acc = g * cw_ref[FFN_CONV - 1:FFN_CONV, :]
            for s in range(1, FFN_CONV):
                rolled = pltpu.roll(g, s, 0)
                head = jnp.where(first_rows < s, pltpu.roll(prev, s, 0), rolled[0:halo])
                shifted = jnp.concatenate([head, rolled[halo:]], axis=0)
                acc = acc + shifted * cw_ref[FFN_CONV - 1 - s:FFN_CONV - s, :]
            prev = g[rows - halo:rows]
            act_ref[sl, :] = _bf(_silu(acc) * u)
        buf[...] = prev
        tail_ref[...] = prev


def _ffn(h2, wg, wu, cw, state, t, b):
    m, d = h2.shape
    f = wg.shape[1]
    tf = 256
    tm = _row_tile(t, 2048)
    npt = t // tm
    assert b <= tm and m == t + b and t % b == 0
    sspec = pl.BlockSpec((b, tf), lambda c, i: (0, c))
    nf = f // tf

    def weight_spec(switch_step):
        return pl.BlockSpec(
            (d, tf), lambda c, i: (0, jnp.minimum(c + (i >= switch_step).astype(jnp.int32), nf - 1)))

    return pl.pallas_call(
        functools.partial(_ffn_kernel, tm=tm, bs=b),
        grid=(nf, npt + 1),
        in_specs=[
            pl.BlockSpec((tm, d), lambda c, i: (jnp.maximum(i - 1, 0), 0)),
            pl.BlockSpec((b, d), lambda c, i: (t // b, 0)),
            weight_spec(2),
            weight_spec(min(4, npt)),
            pl.BlockSpec((FFN_CONV, tf), lambda c, i: (0, c)),
            sspec, sspec,
        ],
        out_specs=[pl.BlockSpec((tm, tf), lambda c, i: (jnp.where(i == 0, npt, i - 1), c)),
                   pl.BlockSpec((SUBLANES, tf), lambda c, i: (0, c)),
                   pl.BlockSpec((b, tf), lambda c, i: (0, c))],
        out_shape=[jax.ShapeDtypeStruct((m, f), BF16),
                   jax.ShapeDtypeStruct((SUBLANES, f), F32),
                   jax.ShapeDtypeStruct((b, f), F32)],
        scratch_shapes=[pltpu.VMEM((d, tf), BF16), pltpu.VMEM((d, tf), BF16),
                        pltpu.VMEM((SUBLANES, tf), F32)],
        compiler_params=_cparams(("arbitrary", "arbitrary"), 60),
        name="ffn",
    )(h2, h2, wg, wu, cw, state[:, 0], state[:, 1])


def _rope_tables(t, b):
    inv = ROPE_THETA ** (-jnp.arange(0, ROT_DIM, 2, dtype=F32) / ROT_DIM)
    pos = jnp.concatenate([jnp.arange(t, dtype=jnp.int32),
                           jnp.full((b,), PAST_LEN, jnp.int32)]).astype(F32)
    ang = pos[:, None] * inv[None, :]
    cos, sin = jnp.cos(ang), jnp.sin(ang)
    m = t + b
    pad = LANES - ROT_DIM
    cos_t = jnp.concatenate([cos, cos, jnp.ones((m, pad), F32)], axis=1)
    sin_t = jnp.concatenate([-sin, sin, jnp.zeros((m, pad), F32)], axis=1)
    return cos_t, sin_t


def kernel(x_prompt, x_sample, cache_kv_w128, cache_kv_w512, cache_kv_w2048, state_conv_qkv, state_delta,
           state_ffn_conv, norm_mix, w_in, conv_qkv, a_log, dt_bias, delta_norm, w_a_out, w_b_out, w_out,
           norm_ffn, w_gate, ffn_conv, w_up, w_down, norm_final):
    assert norm_mix.shape[0] == 1, "single layer only"
    assert x_prompt.shape[0] == 1 and x_sample.shape[1] == 1
    t, d = x_prompt.shape[1], x_prompt.shape[2]
    b = x_sample.shape[0]
    assert t % SUPER == 0 and b == ROW_BLOCK
    caches = (cache_kv_w128[0], cache_kv_w512[0], cache_kv_w2048[0])
    for (win, _), c in zip(DIL_GROUPS, caches):
        assert c.shape[1] == win, "cache must hold a full window"

    off_bqkv = A_COLS
    off_bz = off_bqkv + B_CONV_COLS
    off_ba = off_bz + B_WIDTH
    off_gate = off_ba + 2 * B_HEADS
    assert w_in.shape[2] == off_gate + 2 * d
    w_in_t = jnp.swapaxes(w_in[0], 0, 1)
    w_main = _bf(w_in_t)
    w_ba = jnp.pad(w_in_t[off_ba:off_gate], ((0, LANES - 2 * B_HEADS), (0, 0)))
    w_ba_hi = _bf(w_ba)
    w_ba = jnp.concatenate([w_ba_hi, _bf(w_ba - w_ba_hi.astype(F32))], axis=0)
    gate_col0 = off_ba

    xp = x_prompt.reshape(t, d)
    xs = x_sample.reshape(b, d)
    h, ba = _norm_in(xp, xs, norm_mix, w_ba)
    cos_t, sin_t = _rope_tables(t, b)
    proj = _proj(h, w_main, cos_t, sin_t, gate_col0, off_gate)

    oa_p = _attn_prompt(proj, t)
    oa_s = _attn_sample(proj, caches, t, b)

    alog_row = jnp.pad(a_log[0], (B_HEADS, LANES - 2 * B_HEADS)).reshape(1, LANES)
    dtb_row = jnp.pad(dt_bias[0], (B_HEADS, LANES - 2 * B_HEADS)).reshape(1, LANES)
    ltri = jnp.tril(jnp.ones((CHUNK, CHUNK), F32))
    head_of_col = jnp.arange(2 * B_WIDTH, dtype=jnp.int32) // LANES
    expand = (jnp.arange(LANES, dtype=jnp.int32)[:, None] == head_of_col[None, :]).astype(F32)
    cw_qkv = conv_qkv[0]
    ob_p, delta_p = _delta_prompt(proj, ba, alog_row, dtb_row, ltri, cw_qkv, delta_norm, t, off_bqkv, off_bz)
    beta_s, g_s = _beta_decay(ba, alog_row, dtb_row, ltri, expand, t, b, False)
    qkv_s = _conv_sample(proj, state_conv_qkv[0], cw_qkv, t, b, off_bqkv)
    ob_s, delta_s = _delta_sample(qkv_s, beta_s, g_s, proj, delta_norm, state_delta[0], t, b, off_bz)

    merged = _merge(oa_p, oa_s, ob_p, ob_s, _bf(w_a_out[0]), _bf(w_b_out[0]), proj, gate_col0)
    x1, w_down_bf = _out_proj(merged, _bf(w_out[0]), xp, xs, w_down[0])

    h2 = _norm_mid(x1, norm_ffn)
    act, g_tail, g_smp = _ffn(h2, w_gate[0], w_up[0], ffn_conv[0], state_ffn_conv[0], t, b)
    x2 = _mm_res(act, w_down_bf, x1, 640, 512, "down_proj")
    y_p, y_s = _norm_out(x2, norm_final.reshape(1, d), t, b)

    kv_p, kv_s = [], []
    for g, (win, _) in enumerate(DIL_GROUPS):
        c0 = g * 3 * A_WIDTH + A_WIDTH
        keep = min(win, t)
        kv_p.append(proj[t - keep:t, c0:c0 + 2 * A_WIDTH].reshape(1, 1, keep, 2, A_HEADS, HEAD_DIM))
        kv_s.append(proj[t:, c0:c0 + 2 * A_WIDTH].reshape(1, b, 1, 2, A_HEADS, HEAD_DIM))
    conv_p = proj[t - (B_CONV - 1):t, off_bqkv:off_bz].reshape(1, 1, B_CONV - 1, B_CONV_COLS)
    conv_s = jnp.concatenate([state_conv_qkv[0][:, 1:], proj[t:, off_bqkv:off_bz][:, None]], axis=1)[None]
    ffn_p = g_tail[SUBLANES - (FFN_CONV - 1):].reshape(1, 1, FFN_CONV - 1, -1)
    ffn_s = jnp.concatenate([state_ffn_conv[0][:, 1:], g_smp[:, None]], axis=1)[None]
    return (y_p.reshape(1, t, d), y_s.reshape(b, 1, d), kv_p[0], kv_p[1], kv_p[2],
            conv_p, delta_p.reshape(1, 1, B_HEADS, B_DK, B_DV), ffn_p,
            kv_s[0], kv_s[1], kv_s[2], conv_s, delta_s[None], ffn_s)
```
